```python
import math
import jax, jax.numpy as jnp
from jax import lax
import numpy as np

D_MODEL = 1024
BATCH = 8
SEQ = 2048
DEPTH = 4
DEC_BATCH = 128
DEC_SEQ = 4
PAST_LEN = 8192
PAGE_SIZE = 128

N_META = 16
HEAD_DIM = 64
N_Q_HEADS = 8
N_KV_HEADS = 2
Q_PER_KV = N_Q_HEADS // N_KV_HEADS
ATTN_WIDTH = N_Q_HEADS * HEAD_DIM
KV_WIDTH = N_KV_HEADS * HEAD_DIM
WINDOW = 128
ATTN_BLOCK = 128
ROPE_THETA = 10000.0
D_RNN = D_MODEL
N_LRU_BLOCKS = 8
LRU_BLOCK = D_RNN // N_LRU_BLOCKS
CONV_W = 4
LRU_C = 8.0
D_FF = -(-8 * D_MODEL // (3 * 256)) * 256
LN_EPS = 1e-5
NEG_INF = -1e30
DEEPNORM_ALPHA = (2 * DEPTH) ** 0.25
DEEPNORM_BETA = (8 * DEPTH) ** -0.25

Q_END = ATTN_WIDTH
K_END = Q_END + KV_WIDTH
V_END = K_END + KV_WIDTH
XR_END = V_END + D_RNN
GATE_END = XR_END + D_RNN
GA_END = GATE_END + D_MODEL
D_IN = GA_END + D_MODEL

kernel_name = 'griffin_swa_sink_rglru_deepnorm_meta_step'


def layer_norm(x, g, b):
    x32 = x.astype(jnp.float32)
    mu = x32.mean(-1, keepdims=True)
    var = jnp.square(x32 - mu).mean(-1, keepdims=True)
    y = (x32 - mu) * lax.rsqrt(var + LN_EPS) * g.astype(jnp.float32) + b.astype(jnp.float32)
    return y.astype(x.dtype)


def rope(x, pos):
    half = HEAD_DIM // 2
    inv = ROPE_THETA ** (-jnp.arange(half, dtype=jnp.float32) / half)
    ang = pos.astype(jnp.float32)[:, None] * inv[None, :]
    cos = jnp.cos(ang)[None, :, None, :]
    sin = jnp.sin(ang)[None, :, None, :]
    x32 = x.astype(jnp.float32)
    x1, x2 = x32[..., :half], x32[..., half:]
    return jnp.concatenate([x1 * cos - x2 * sin, x2 * cos + x1 * sin], axis=-1).astype(x.dtype)


def sink_attention(q, k, v, valid, sinks):
    s = jnp.einsum('bnqkgd,bnskd->bnkgqs', q, k, preferred_element_type=jnp.float32) * (HEAD_DIM ** -0.5)
    s = jnp.where(valid[None, :, None, None], s, NEG_INF)
    sink = sinks.astype(jnp.float32).reshape(N_KV_HEADS, Q_PER_KV)[None, None, :, :, None, None]
    m = jnp.maximum(s.max(-1, keepdims=True), sink)
    p = jnp.exp(s - m)
    denom = p.sum(-1, keepdims=True) + jnp.exp(sink - m)
    return jnp.einsum('bnkgqs,bnskd->bnqkgd', (p / denom).astype(v.dtype), v)


def window_mask(qpos, kpos):
    qp = qpos[..., :, None]
    kp = kpos[..., None, :]
    return (kp <= qp) & (kp > qp - WINDOW) & (kp >= 0)


def attn_prompt(q, k, v, sinks):
    B, L = q.shape[:2]
    pad = (-L) % ATTN_BLOCK
    Lp = L + pad
    nb = Lp // ATTN_BLOCK

    def padf(t):
        return jnp.pad(t, ((0, 0), (pad, 0)) + ((0, 0),) * (t.ndim - 2))

    def with_prev(t):
        prev = jnp.concatenate([jnp.zeros_like(t[:, :1]), t[:, :-1]], axis=1)
        return jnp.concatenate([prev, t], axis=2)

    qb = padf(q).reshape(B, nb, ATTN_BLOCK, N_KV_HEADS, Q_PER_KV, HEAD_DIM)
    kb = with_prev(padf(k).reshape(B, nb, ATTN_BLOCK, N_KV_HEADS, HEAD_DIM))
    vb = with_prev(padf(v).reshape(B, nb, ATTN_BLOCK, N_KV_HEADS, HEAD_DIM))
    qpos = (jnp.arange(Lp, dtype=jnp.int32) - pad).reshape(nb, ATTN_BLOCK)
    kpos = jnp.concatenate([qpos - ATTN_BLOCK, qpos], axis=1)
    o = sink_attention(qb, kb, vb, window_mask(qpos, kpos), sinks)
    return o.reshape(B, Lp, ATTN_WIDTH)[:, pad:]


def attn_sample(q, k_new, v_new, k_buf, v_buf, sinks):
    DB, S = q.shape[:2]
    W = k_buf.shape[1]
    kk = jnp.concatenate([k_buf.astype(k_new.dtype), k_new], axis=1)
    vv = jnp.concatenate([v_buf.astype(v_new.dtype), v_new], axis=1)
    qpos = PAST_LEN + jnp.arange(S, dtype=jnp.int32)
    kpos = jnp.concatenate([PAST_LEN - W + jnp.arange(W, dtype=jnp.int32), qpos])
    valid = window_mask(qpos, kpos)[None]
    o = sink_attention(q.reshape(DB, 1, S, N_KV_HEADS, Q_PER_KV, HEAD_DIM), kk[:, None], vv[:, None], valid, sinks)
    return o.reshape(DB, S, ATTN_WIDTH), kk[:, S:], vv[:, S:]


def causal_conv(xr, buf, w, b):
    T = xr.shape[1]
    xe = jnp.concatenate([buf.astype(xr.dtype), xr], axis=1)
    y = sum(xe[:, j:j + T] * w[j] for j in range(CONV_W)) + b
    return y, xe[:, -(CONV_W - 1):]


def _lin_combine(c1, c2):
    a1, b1 = c1
    a2, b2 = c2
    return a1 * a2, a2 * b1 + b2


def rg_lru(xc, h0, wa, ba, wx, bx, lam):
    B, T, _ = xc.shape
    x32 = xc.astype(jnp.float32)
    xb = x32.reshape(B, T, N_LRU_BLOCKS, LRU_BLOCK)
    r = jax.nn.sigmoid(jnp.einsum('btnc,ncd->btnd', xb, wa.astype(jnp.float32)).reshape(B, T, D_RNN) + ba.astype(jnp.float32))
    i = jax.nn.sigmoid(jnp.einsum('btnc,ncd->btnd', xb, wx.astype(jnp.float32)).reshape(B, T, D_RNN) + bx.astype(jnp.float32))
    log_a = -LRU_C * r * jax.nn.softplus(-lam.astype(jnp.float32))
    a = jnp.exp(log_a)
    b = jnp.sqrt(-jnp.expm1(2.0 * log_a)) * (i * x32)
    b = b.at[:, 0].add(a[:, 0] * h0.astype(jnp.float32))
    _, h = lax.associative_scan(_lin_combine, (a, b), axis=1)
    return h, h[:, -1]


def mixer_block(x, pos, l, p, k_buf, v_buf, conv_buf, h0):
    B, T, _ = x.shape
    proj = x @ p['w_in'][l]
    q = rope(proj[..., :Q_END].reshape(B, T, N_Q_HEADS, HEAD_DIM), pos)
    k = rope(proj[..., Q_END:K_END].reshape(B, T, N_KV_HEADS, HEAD_DIM), pos)
    v = proj[..., K_END:V_END].reshape(B, T, N_KV_HEADS, HEAD_DIM)
    xr = proj[..., V_END:XR_END]
    gate = proj[..., XR_END:GATE_END]
    g_attn = proj[..., GATE_END:GA_END]
    g_lru = proj[..., GA_END:]
    sinks = p['attn_sinks'][l]
    if k_buf is None:
        attn = attn_prompt(q, k, v, sinks)
        new_k, new_v = k[:, -WINDOW:], v[:, -WINDOW:]
    else:
        attn, new_k, new_v = attn_sample(q, k, v, k_buf, v_buf, sinks)
    xc, new_conv = causal_conv(xr, conv_buf, p['conv_w'][l], p['conv_b'][l])
    h, h_last = rg_lru(xc, h0, p['lru_wa'][l], p['lru_ba'][l], p['lru_wx'][l], p['lru_bx'][l], p['lru_lambda'][l])
    rec = h.astype(x.dtype) * jax.nn.gelu(gate)
    merged = (jax.nn.sigmoid(g_attn) * (attn @ p['w_attn_proj'][l])
              + jax.nn.sigmoid(g_lru) * (rec @ p['w_lru_proj'][l]))
    return merged @ p['w_out'][l], (new_k, new_v, new_conv, h_last.astype(x.dtype))


def swiglu(x, w_in, w_out):
    u = x @ w_in
    return (jax.nn.silu(u[..., :D_FF]) * u[..., D_FF:]) @ w_out


def trunk(x, pos, p, cache_k, cache_v, conv_state, lru_state):
    B = x.shape[0]
    outs = ([], [], [], [])
    for l in range(DEPTH):
        if cache_k is None:
            kb = vb = None
            cb = jnp.zeros((B, CONV_W - 1, D_RNN), x.dtype)
            h0 = jnp.zeros((B, D_RNN), x.dtype)
        else:
            kb, vb, cb, h0 = cache_k[l], cache_v[l], conv_state[l], lru_state[l]
        mix, st = mixer_block(x, pos, l, p, kb, vb, cb, h0)
        x = layer_norm(DEEPNORM_ALPHA * x + mix, p['ln1_g'][l], p['ln1_b'][l])
        x = layer_norm(DEEPNORM_ALPHA * x + swiglu(x, p['w_ffn_in'][l], p['w_ffn_out'][l]), p['ln2_g'][l], p['ln2_b'][l])
        for o, s in zip(outs, st):
            o.append(s)
    return x, [jnp.stack(o) for o in outs]


def setup_inputs(seed: int = 0) -> dict:
    key = jax.random.key(seed)
    ks = jax.random.split(key, 32)
    f32 = jnp.float32

    def nrm(k, shape, scale=1.0):
        return jax.random.normal(k, shape, f32) * scale

    a_c = jax.random.uniform(ks[14], (DEPTH, D_RNN), f32, 0.9, 0.999)
    a_base = a_c ** (1.0 / LRU_C)
    lru_lambda = jnp.log(a_base) - jnp.log1p(-a_base)
    return {
        'x_prompt': nrm(ks[0], (BATCH, SEQ, D_MODEL)),
        'x_sample': nrm(ks[1], (DEC_BATCH, DEC_SEQ, D_MODEL)),
        'cache_win_k': nrm(ks[2], (DEPTH, DEC_BATCH, WINDOW, N_KV_HEADS, HEAD_DIM)),
        'cache_win_v': nrm(ks[3], (DEPTH, DEC_BATCH, WINDOW, N_KV_HEADS, HEAD_DIM)),
        'state_conv': nrm(ks[4], (DEPTH, DEC_BATCH, CONV_W - 1, D_RNN)),
        'state_lru': nrm(ks[5], (DEPTH, DEC_BATCH, D_RNN), 0.5),
        'meta_tokens': nrm(ks[6], (N_META, D_MODEL)),
        'w_in': nrm(ks[7], (DEPTH, D_MODEL, D_IN), D_MODEL ** -0.5),
        'w_attn_proj': nrm(ks[8], (DEPTH, ATTN_WIDTH, D_MODEL), ATTN_WIDTH ** -0.5),
        'w_lru_proj': nrm(ks[9], (DEPTH, D_RNN, D_MODEL), D_RNN ** -0.5),
        'w_out': nrm(ks[10], (DEPTH, D_MODEL, D_MODEL), DEEPNORM_BETA * D_MODEL ** -0.5),
        'attn_sinks': nrm(ks[11], (DEPTH, N_Q_HEADS), 0.5),
        'conv_w': nrm(ks[12], (DEPTH, CONV_W, D_RNN), CONV_W ** -0.5),
        'conv_b': nrm(ks[13], (DEPTH, D_RNN), 0.01),
        'lru_wa': nrm(ks[15], (DEPTH, N_LRU_BLOCKS, LRU_BLOCK, LRU_BLOCK), LRU_BLOCK ** -0.5),
        'lru_ba': nrm(ks[16], (DEPTH, D_RNN), 0.01),
        'lru_wx': nrm(ks[17], (DEPTH, N_LRU_BLOCKS, LRU_BLOCK, LRU_BLOCK), LRU_BLOCK ** -0.5),
        'lru_bx': nrm(ks[18], (DEPTH, D_RNN), 0.01),
        'lru_lambda': lru_lambda,
        'ln1_g': 1.0 + nrm(ks[19], (DEPTH, D_MODEL), 0.01),
        'ln1_b': nrm(ks[20], (DEPTH, D_MODEL), 0.01),
        'w_ffn_in': nrm(ks[21], (DEPTH, D_MODEL, 2 * D_FF), D_MODEL ** -0.5),
        'w_ffn_out': nrm(ks[22], (DEPTH, D_FF, D_MODEL), DEEPNORM_BETA * D_FF ** -0.5),
        'ln2_g': 1.0 + nrm(ks[23], (DEPTH, D_MODEL), 0.01),
        'ln2_b': nrm(ks[24], (DEPTH, D_MODEL), 0.01),
    }


def reference(x_prompt, x_sample, cache_win_k, cache_win_v, state_conv, state_lru,
              meta_tokens, w_in, w_attn_proj, w_lru_proj, w_out, attn_sinks,
              conv_w, conv_b, lru_wa, lru_ba, lru_wx, lru_bx, lru_lambda,
              ln1_g, ln1_b, w_ffn_in, w_ffn_out, ln2_g, ln2_b):
    p = dict(w_in=w_in, w_attn_proj=w_attn_proj, w_lru_proj=w_lru_proj, w_out=w_out,
             attn_sinks=attn_sinks, conv_w=conv_w, conv_b=conv_b, lru_wa=lru_wa, lru_ba=lru_ba,
             lru_wx=lru_wx, lru_bx=lru_bx, lru_lambda=lru_lambda, ln1_g=ln1_g, ln1_b=ln1_b,
             w_ffn_in=w_ffn_in, w_ffn_out=w_ffn_out, ln2_g=ln2_g, ln2_b=ln2_b)
    B, T, D = x_prompt.shape
    meta = jnp.broadcast_to(meta_tokens.astype(x_prompt.dtype)[None], (B, N_META, D))
    xp = jnp.concatenate([meta, x_prompt], axis=1)
    pos_p = jnp.arange(T + N_META, dtype=jnp.int32)
    yp, st_p = trunk(xp, pos_p, p, None, None, None, None)
    pos_s = PAST_LEN + jnp.arange(x_sample.shape[1], dtype=jnp.int32)
    ys, st_s = trunk(x_sample, pos_s, p, cache_win_k, cache_win_v, state_conv, state_lru)
    return (yp[:, N_META:], ys, st_p[0], st_p[1], st_p[2], st_p[3], st_s[0], st_s[1], st_s[2], st_s[3])
```

```python
import functools

import jax
import jax.numpy as jnp
from jax import lax
from jax.experimental import pallas as pl
from jax.experimental.pallas import tpu as pltpu

D_MODEL = 1024
N_META = 16
HEAD_DIM = 64
N_Q_HEADS = 8
N_KV_HEADS = 2
Q_PER_KV = N_Q_HEADS // N_KV_HEADS
ATTN_WIDTH = N_Q_HEADS * HEAD_DIM
KV_WIDTH = N_KV_HEADS * HEAD_DIM
WINDOW = 128
ROPE_THETA = 10000.0
D_RNN = D_MODEL
N_LRU_BLOCKS = 8
LRU_BLOCK = D_RNN // N_LRU_BLOCKS
CONV_W = 4
LRU_C = 8.0
D_FF = -(-8 * D_MODEL // (3 * 256)) * 256
LN_EPS = 1e-5
NEG_INF = -1e30
PAST_LEN = 8192

Q_END = ATTN_WIDTH
K_END = Q_END + KV_WIDTH
V_END = K_END + KV_WIDTH
XR_END = V_END + D_RNN
GATE_END = XR_END + D_RNN
GA_END = GATE_END + D_MODEL

LANES = 128
VMEM_LIMIT_BYTES = 56 * 1024 * 1024

F32 = jnp.float32
BF16 = jnp.bfloat16


def _dot(a, b):
    return jnp.dot(a, b, preferred_element_type=F32)


def _layer_norm(y, g, b):
    mu = jnp.mean(y, axis=-1, keepdims=True)
    d = y - mu
    var = jnp.mean(d * d, axis=-1, keepdims=True)
    return d * lax.rsqrt(var + LN_EPS) * g + b


def _rope(x, cos, sin_signed):
    lane = lax.broadcasted_iota(jnp.int32, (x.shape[0], LANES), 1)
    first_half = (lane & (HEAD_DIM - 1)) < (HEAD_DIM // 2)
    outs = []
    for g in range(x.shape[1] // LANES):
        xg = x[:, g * LANES:(g + 1) * LANES]
        partner = jnp.where(first_half,
                            pltpu.roll(xg, LANES - HEAD_DIM // 2, 1),
                            pltpu.roll(xg, HEAD_DIM // 2, 1))
        outs.append(xg * cos + partner * sin_signed)
    return outs[0] if len(outs) == 1 else jnp.concatenate(outs, axis=1)


def _softmax_pv(s, valid, sink_col, vv):
    s = jnp.where(valid, s, NEG_INF)
    m = jnp.maximum(jnp.max(s, axis=-1, keepdims=True), sink_col)
    p = jnp.exp(s - m)
    denom = jnp.sum(p, axis=-1, keepdims=True) + jnp.exp(sink_col - m)
    return _dot(p.astype(BF16), vv) * (1.0 / denom)


def _sink_column(sink_ref, kv, tt):
    row = lax.broadcasted_iota(jnp.int32, (Q_PER_KV * tt, 1), 0)
    col = jnp.full((Q_PER_KV * tt, 1), sink_ref[0, Q_PER_KV * kv + Q_PER_KV - 1], F32)
    for g in range(Q_PER_KV - 2, -1, -1):
        col = jnp.where(row < (g + 1) * tt, sink_ref[0, Q_PER_KV * kv + g], col)
    return col


def _window_valid(tt, w, jmin):
    t_idx = lax.broadcasted_iota(jnp.int32, (Q_PER_KV * tt, w), 0) & (tt - 1)
    j_idx = lax.broadcasted_iota(jnp.int32, (Q_PER_KV * tt, w), 1)
    return (j_idx > t_idx) & (j_idx <= t_idx + WINDOW) & (j_idx >= jmin)


def _recurrent_branch(xr, gate, xe_ref, a_ref, b_ref, hs_ref, h_ref,
                      convw_ref, convb_ref, wax_ref, ba_ref, bx_ref, lam_ref, *, ns, tt):
    r_rows = tt * ns
    hist = (CONV_W - 1) * ns
    xe_ref[hist:hist + r_rows, :] = xr
    xc = convb_ref[...] + xe_ref[0:r_rows, :] * convw_ref[0:1, :]
    for j in range(1, CONV_W):
        xc = xc + xe_ref[j * ns:j * ns + r_rows, :] * convw_ref[j:j + 1, :]
    new_hist = xe_ref[r_rows:r_rows + hist, :]
    xe_ref[0:hist, :] = new_hist

    xcb = xc.astype(BF16)
    r_parts, i_parts = [], []
    for n in range(N_LRU_BLOCKS):
        ri = _dot(xcb[:, n * LRU_BLOCK:(n + 1) * LRU_BLOCK], wax_ref[n])
        r_parts.append(ri[:, :LRU_BLOCK])
        i_parts.append(ri[:, LRU_BLOCK:])
    r_gate = jax.nn.sigmoid(jnp.concatenate(r_parts, axis=1) + ba_ref[...])
    i_gate = jax.nn.sigmoid(jnp.concatenate(i_parts, axis=1) + bx_ref[...])
    log_a = (-LRU_C) * r_gate * jax.nn.softplus(-lam_ref[...])
    a = jnp.exp(log_a)
    a_ref[...] = a
    b_ref[...] = jnp.sqrt(-jnp.tanh(log_a) * (a * a + 1.0)) * (i_gate * xc)

    h = h_ref[...]
    for t in range(tt):
        h = a_ref[t * ns:(t + 1) * ns, :] * h + b_ref[t * ns:(t + 1) * ns, :]
        hs_ref[t * ns:(t + 1) * ns, :] = h
    h_ref[...] = h
    return hs_ref[...] * jax.nn.gelu(gate)


def _merge_norm(x, xb, attn, rec, wg_ref, wap_ref, wlp_ref, wout_ref, g1_ref, b1_ref, alpha):
    gates = _dot(xb, wg_ref[...])
    ap = _dot(attn.astype(BF16), wap_ref[...])
    lp = _dot(rec.astype(BF16), wlp_ref[...])
    merged = (jax.nn.sigmoid(gates[:, :D_MODEL]) * ap
              + jax.nn.sigmoid(gates[:, D_MODEL:]) * lp)
    mix = _dot(merged.astype(BF16), wout_ref[...])
    return _layer_norm(alpha * x + mix, g1_ref[...], b1_ref[...])


def _mixer_kernel(sink_ref, x_ref, cos_ref, sin_ref,
                  wqkv_ref, wxg_ref, wg_ref, wap_ref, wlp_ref, wout_ref,
                  convw_ref, convb_ref, wax_ref, ba_ref, bx_ref, lam_ref, g1_ref, b1_ref,
                  kwin0_ref, vwin0_ref, conv0_ref, h0_ref,
                  y_ref, kwin_out_ref, vwin_out_ref, conv_out_ref, h_out_ref,
                  q_s, k_s, v_s, attn_s, kwin_s, vwin_s, xe_s, a_s, b_s, hs_s, h_s,
                  *, ns, tt, pos0, alpha):
    step = pl.program_id(0)
    r_rows = ns * tt
    hist = (CONV_W - 1) * ns

    @pl.when(step == 0)
    def _():
        kwin_s[:, 0:WINDOW, :] = kwin0_ref[...]
        vwin_s[:, 0:WINDOW, :] = vwin0_ref[...]
        xe_s[0:hist, :] = conv0_ref[...]
        h_s[...] = h0_ref[...]

    xb = x_ref[...].astype(BF16)

    qkv = _dot(xb, wqkv_ref[...])
    cos = cos_ref[...]
    sin = sin_ref[...]
    n_qg = ATTN_WIDTH // LANES
    for j in range(n_qg):
        q_s[j] = _rope(qkv[:, j * LANES:(j + 1) * LANES], cos, sin)
    k_s[...] = _rope(qkv[:, Q_END:K_END], cos, sin)
    v_s[...] = qkv[:, K_END:V_END]

    jmin = WINDOW - (pos0 + step * tt)
    valid = _window_valid(tt, WINDOW + tt, jmin)
    sink_cols = [_sink_column(sink_ref, kv, tt) for kv in range(N_KV_HEADS)]
    for b in range(ns):
        rows = pl.ds(b, tt, stride=ns)
        kwin_s[b, WINDOW:WINDOW + tt, :] = k_s[rows, :]
        vwin_s[b, WINDOW:WINDOW + tt, :] = v_s[rows, :]
        keys = kwin_s[b]
        vals = vwin_s[b]
        kb = keys.astype(BF16)
        vb = vals.astype(BF16)
        q_b = [q_s[j, rows, :].astype(BF16) for j in range(n_qg)]
        heads = [q_b[h // 2][:, (h % 2) * HEAD_DIM:(h % 2 + 1) * HEAD_DIM] for h in range(N_Q_HEADS)]
        outs = []
        for kv in range(N_KV_HEADS):
            kk = kb[:, kv * HEAD_DIM:(kv + 1) * HEAD_DIM]
            vv = vb[:, kv * HEAD_DIM:(kv + 1) * HEAD_DIM]
            qs = jnp.concatenate(heads[Q_PER_KV * kv:Q_PER_KV * (kv + 1)], axis=0)
            s = lax.dot_general(qs, kk, (((1,), (1,)), ((), ())),
                                preferred_element_type=F32) * (HEAD_DIM ** -0.5)
            o = _softmax_pv(s, valid, sink_cols[kv], vv)
            outs.extend(o[g * tt:(g + 1) * tt] for g in range(Q_PER_KV))
        for j in range(n_qg):
            attn_s[j, rows, :] = jnp.concatenate(outs[2 * j:2 * j + 2], axis=1)
        kwin_s[b, 0:WINDOW, :] = keys[tt:tt + WINDOW]
        vwin_s[b, 0:WINDOW, :] = vals[tt:tt + WINDOW]

    xg = _dot(xb, wxg_ref[...])
    rec = _recurrent_branch(xg[:, :D_RNN], xg[:, D_RNN:], xe_s, a_s, b_s, hs_s, h_s,
                            convw_ref, convb_ref, wax_ref, ba_ref, bx_ref, lam_ref,
                            ns=ns, tt=tt)

    attn = jnp.concatenate([attn_s[j] for j in range(n_qg)], axis=1)
    y_ref[...] = _merge_norm(x_ref[...], xb, attn, rec, wg_ref, wap_ref, wlp_ref, wout_ref,
                             g1_ref, b1_ref, alpha)

    @pl.when(step == pl.num_programs(0) - 1)
    def _():
        kwin_out_ref[...] = kwin_s[:, 0:WINDOW, :]
        vwin_out_ref[...] = vwin_s[:, 0:WINDOW, :]
        conv_out_ref[...] = xe_s[0:hist, :]
        h_out_ref[...] = h_s[...]


def _const_spec(shape):
    nd = len(shape)
    return pl.BlockSpec(shape, lambda i, _nd=nd: (0,) * _nd, pipeline_mode=pl.Buffered(1))


def _mixer_call(x, cos, sin, lw, state, *, ns, tt, pos0, alpha):
    rows = x.shape[0]
    r_rows = ns * tt
    n_tiles = rows // r_rows
    hist = (CONV_W - 1) * ns
    kwin0, vwin0, conv0, h0 = state

    def row_spec(c):
        return pl.BlockSpec((r_rows, c), lambda i: (i, 0))

    weights = [lw['w_qkv'], lw['w_xg'], lw['w_g'], lw['w_attn_proj'], lw['w_lru_proj'], lw['w_out'],
               lw['conv_w'], lw['conv_b'], lw['w_ax'], lw['lru_ba'], lw['lru_bx'], lw['lru_lambda'],
               lw['ln1_g'], lw['ln1_b']]
    in_specs = ([pl.BlockSpec(memory_space=pltpu.SMEM), row_spec(D_MODEL), row_spec(LANES), row_spec(LANES)]
                + [_const_spec(w.shape) for w in weights]
                + [_const_spec(s.shape) for s in (kwin0, vwin0, conv0, h0)])
    out_shape = (jax.ShapeDtypeStruct((rows, D_MODEL), F32),
                 jax.ShapeDtypeStruct((ns, WINDOW, KV_WIDTH), F32),
                 jax.ShapeDtypeStruct((ns, WINDOW, KV_WIDTH), F32),
                 jax.ShapeDtypeStruct((hist, D_RNN), F32),
                 jax.ShapeDtypeStruct((ns, D_RNN), F32))
    out_specs = (row_spec(D_MODEL),
                 pl.BlockSpec((ns, WINDOW, KV_WIDTH), lambda i: (0, 0, 0)),
                 pl.BlockSpec((ns, WINDOW, KV_WIDTH), lambda i: (0, 0, 0)),
                 pl.BlockSpec((hist, D_RNN), lambda i: (0, 0)),
                 pl.BlockSpec((ns, D_RNN), lambda i: (0, 0)))
    scratch = [pltpu.VMEM((ATTN_WIDTH // LANES, r_rows, LANES), F32),
               pltpu.VMEM((r_rows, KV_WIDTH), F32),
               pltpu.VMEM((r_rows, KV_WIDTH), F32),
               pltpu.VMEM((ATTN_WIDTH // LANES, r_rows, LANES), F32),
               pltpu.VMEM((ns, WINDOW + tt, KV_WIDTH), F32),
               pltpu.VMEM((ns, WINDOW + tt, KV_WIDTH), F32),
               pltpu.VMEM((hist + r_rows, D_RNN), F32),
               pltpu.VMEM((r_rows, D_RNN), F32),
               pltpu.VMEM((r_rows, D_RNN), F32),
               pltpu.VMEM((r_rows, D_RNN), F32),
               pltpu.VMEM((ns, D_RNN), F32)]
    outs = pl.pallas_call(
        functools.partial(_mixer_kernel, ns=ns, tt=tt, pos0=pos0, alpha=alpha),
        grid=(n_tiles,),
        in_specs=in_specs,
        out_specs=out_specs,
        out_shape=out_shape,
        scratch_shapes=scratch,
        compiler_params=pltpu.CompilerParams(dimension_semantics=("arbitrary",),
                                             vmem_limit_bytes=VMEM_LIMIT_BYTES),
        name=f"mixer_ns{ns}_tt{tt}",
    )(lw['sinks'], x, cos, sin, *weights, kwin0, vwin0, conv0, h0)
    return outs[0], outs[1:]


def _ffn_kernel(x_ref, win_ref, wout_ref, g_ref, b_ref, y_ref, *, alpha):
    x = x_ref[...]
    u = _dot(x.astype(BF16), win_ref[...])
    hmid = jax.nn.silu(u[:, :D_FF]) * u[:, D_FF:]
    f = _dot(hmid.astype(BF16), wout_ref[...])
    y_ref[...] = _layer_norm(alpha * x + f, g_ref[...], b_ref[...])


def _ffn_call(x, lw, *, tm, alpha):
    rows = x.shape[0]
    weights = [lw['w_ffn_in'], lw['w_ffn_out'], lw['ln2_g'], lw['ln2_b']]
    return pl.pallas_call(
        functools.partial(_ffn_kernel, alpha=alpha),
        grid=(rows // tm,),
        in_specs=[pl.BlockSpec((tm, D_MODEL), lambda i: (i, 0))] + [_const_spec(w.shape) for w in weights],
        out_specs=pl.BlockSpec((tm, D_MODEL), lambda i: (i, 0)),
        out_shape=jax.ShapeDtypeStruct((rows, D_MODEL), F32),
        compiler_params=pltpu.CompilerParams(dimension_semantics=("arbitrary",),
                                             vmem_limit_bytes=VMEM_LIMIT_BYTES),
        name=f"ffn_tm{tm}",
    )(x, *weights)


def _sample_pre_kernel(x_ref, cos_ref, sin_ref, wqkv_ref, wxg_ref,
                       convw_ref, convb_ref, wax_ref, ba_ref, bx_ref, lam_ref,
                       conv0_ref, h0_ref,
                       q_ref, k_ref, v_ref, rec_ref, conv_out_ref, h_out_ref,
                       xe_s, a_s, b_s, hs_s, h_s, *, ns, tt):
    hist = (CONV_W - 1) * ns
    xe_s[0:hist, :] = conv0_ref[...]
    h_s[...] = h0_ref[...]
    xb = x_ref[...].astype(BF16)
    qkv = _dot(xb, wqkv_ref[...])
    cos = cos_ref[...]
    sin = sin_ref[...]
    q_ref[...] = _rope(qkv[:, :Q_END], cos, sin)
    k_ref[...] = _rope(qkv[:, Q_END:K_END], cos, sin)
    v_ref[...] = qkv[:, K_END:V_END]
    xg = _dot(xb, wxg_ref[...])
    rec_ref[...] = _recurrent_branch(xg[:, :D_RNN], xg[:, D_RNN:], xe_s, a_s, b_s, hs_s, h_s,
                                     convw_ref, convb_ref, wax_ref, ba_ref, bx_ref, lam_ref,
                                     ns=ns, tt=tt)
    conv_out_ref[...] = xe_s[0:hist, :]
    h_out_ref[...] = h_s[...]


def _sample_pre_call(x, cos, sin, lw, conv0, h0, *, ns, tt):
    rows = ns * tt
    hist = (CONV_W - 1) * ns
    weights = [lw['w_qkv'], lw['w_xg'], lw['conv_w'], lw['conv_b'], lw['w_ax'],
               lw['lru_ba'], lw['lru_bx'], lw['lru_lambda']]
    ins = [x, cos, sin, *weights, conv0, h0]
    out_shape = (jax.ShapeDtypeStruct((rows, ATTN_WIDTH), F32),
                 jax.ShapeDtypeStruct((rows, KV_WIDTH), F32),
                 jax.ShapeDtypeStruct((rows, KV_WIDTH), F32),
                 jax.ShapeDtypeStruct((rows, D_RNN), F32),
                 jax.ShapeDtypeStruct((hist, D_RNN), F32),
                 jax.ShapeDtypeStruct((ns, D_RNN), F32))
    return pl.pallas_call(
        functools.partial(_sample_pre_kernel, ns=ns, tt=tt),
        grid=(1,),
        in_specs=[_const_spec(a.shape) for a in ins],
        out_specs=tuple(pl.BlockSpec(s.shape, lambda i, _n=len(s.shape): (0,) * _n) for s in out_shape),
        out_shape=out_shape,
        scratch_shapes=[pltpu.VMEM((hist + rows, D_RNN), F32),
                        pltpu.VMEM((rows, D_RNN), F32),
                        pltpu.VMEM((rows, D_RNN), F32),
                        pltpu.VMEM((rows, D_RNN), F32),
                        pltpu.VMEM((ns, D_RNN), F32)],
        compiler_params=pltpu.CompilerParams(dimension_semantics=("arbitrary",),
                                             vmem_limit_bytes=VMEM_LIMIT_BYTES),
        name="sample_pre",
    )(*ins)


def _sample_attn_kernel(sink_ref, q_ref, kn_ref, vn_ref, kc_ref, vc_ref,
                        o_ref, ko_ref, vo_ref, kbuf_s, vbuf_s, *, nb, tt):
    w = WINDOW + tt
    valid = _window_valid(tt, w, WINDOW - PAST_LEN)
    sink_cols = [_sink_column(sink_ref, kv, tt) for kv in range(N_KV_HEADS)]
    for s in range(nb):
        kbuf_s[0:WINDOW, :] = kc_ref[s]
        kbuf_s[WINDOW:w, :] = kn_ref[s]
        vbuf_s[0:WINDOW, :] = vc_ref[s]
        vbuf_s[WINDOW:w, :] = vn_ref[s]
        kb = kbuf_s[0:w, :].astype(BF16)
        vb = vbuf_s[0:w, :].astype(BF16)
        for kv in range(N_KV_HEADS):
            kk = kb[:, kv * HEAD_DIM:(kv + 1) * HEAD_DIM]
            vv = vb[:, kv * HEAD_DIM:(kv + 1) * HEAD_DIM]
            qs = q_ref[s, kv].astype(BF16)
            sc = lax.dot_general(qs, kk, (((1,), (1,)), ((), ())),
                                 preferred_element_type=F32) * (HEAD_DIM ** -0.5)
            o_ref[s, kv] = _softmax_pv(sc, valid, sink_cols[kv], vv)
        ko_ref[s] = kbuf_s[tt:w, :]
        vo_ref[s] = vbuf_s[tt:w, :]


def _sample_attn_call(sinks, q, kn, vn, kc, vc, *, nb, tt):
    nseq = q.shape[0]
    pad_w = WINDOW + 8 * (-(-tt // 8))

    def spec(shape):
        blk = (nb,) + tuple(shape[1:])
        nd = len(shape)
        return pl.BlockSpec(blk, lambda i, _nd=nd: (i,) + (0,) * (_nd - 1))

    out_shape = (jax.ShapeDtypeStruct(q.shape, F32),
                 jax.ShapeDtypeStruct(kc.shape, F32),
                 jax.ShapeDtypeStruct(vc.shape, F32))
    return pl.pallas_call(
        functools.partial(_sample_attn_kernel, nb=nb, tt=tt),
        grid=(nseq // nb,),
        in_specs=[pl.BlockSpec(memory_space=pltpu.SMEM)] + [spec(a.shape) for a in (q, kn, vn, kc, vc)],
        out_specs=tuple(spec(s.shape) for s in out_shape),
        out_shape=out_shape,
        scratch_shapes=[pltpu.VMEM((pad_w, KV_WIDTH), F32), pltpu.VMEM((pad_w, KV_WIDTH), F32)],
        compiler_params=pltpu.CompilerParams(dimension_semantics=("arbitrary",),
                                             vmem_limit_bytes=VMEM_LIMIT_BYTES),
        name="sample_attn",
    )(sinks, q, kn, vn, kc, vc)


def _sample_post_kernel(x_ref, attn_ref, rec_ref, wg_ref, wap_ref, wlp_ref, wout_ref,
                        g1_ref, b1_ref, y_ref, *, alpha):
    x = x_ref[...]
    y_ref[...] = _merge_norm(x, x.astype(BF16), attn_ref[...], rec_ref[...],
                             wg_ref, wap_ref, wlp_ref, wout_ref, g1_ref, b1_ref, alpha)


def _sample_post_call(x, attn, rec, lw, *, alpha):
    ins = [x, attn, rec, lw['w_g'], lw['w_attn_proj'], lw['w_lru_proj'], lw['w_out'],
           lw['ln1_g'], lw['ln1_b']]
    return pl.pallas_call(
        functools.partial(_sample_post_kernel, alpha=alpha),
        grid=(1,),
        in_specs=[_const_spec(a.shape) for a in ins],
        out_specs=pl.BlockSpec(x.shape, lambda i: (0, 0)),
        out_shape=jax.ShapeDtypeStruct(x.shape, F32),
        compiler_params=pltpu.CompilerParams(dimension_semantics=("arbitrary",),
                                             vmem_limit_bytes=VMEM_LIMIT_BYTES),
        name="sample_post",
    )(*ins)


def _rope_tables(pos, repeat):
    half = HEAD_DIM // 2
    inv = ROPE_THETA ** (-jnp.arange(half, dtype=F32) / half)
    ang = pos.astype(F32)[:, None] * inv[None, :]
    cos = jnp.cos(ang)
    sin = jnp.sin(ang)
    cos_t = jnp.tile(cos, (1, LANES // half))
    sin_t = jnp.tile(jnp.concatenate([-sin, sin], axis=1), (1, LANES // HEAD_DIM))
    return jnp.repeat(cos_t, repeat, axis=0), jnp.repeat(sin_t, repeat, axis=0)


def _layer_weights(l, w_in, w_attn_proj, w_lru_proj, w_out, attn_sinks, conv_w, conv_b,
                   lru_wa, lru_ba, lru_wx, lru_bx, lru_lambda, ln1_g, ln1_b,
                   w_ffn_in, w_ffn_out, ln2_g, ln2_b):
    wi = w_in[l]
    row = lambda v: v[l].reshape(1, -1).astype(F32)
    return dict(
        w_qkv=wi[:, :V_END].astype(BF16),
        w_xg=wi[:, V_END:GATE_END].astype(BF16),
        w_g=wi[:, GATE_END:].astype(BF16),
        w_attn_proj=w_attn_proj[l].astype(BF16),
        w_lru_proj=w_lru_proj[l].astype(BF16),
        w_out=w_out[l].astype(BF16),
        sinks=attn_sinks[l].reshape(1, N_Q_HEADS).astype(F32),
        conv_w=conv_w[l].astype(F32),
        conv_b=row(conv_b),
        w_ax=jnp.concatenate([lru_wa[l], lru_wx[l]], axis=-1).astype(BF16),
        lru_ba=row(lru_ba), lru_bx=row(lru_bx), lru_lambda=row(lru_lambda),
        ln1_g=row(ln1_g), ln1_b=row(ln1_b),
        w_ffn_in=w_ffn_in[l].astype(BF16),
        w_ffn_out=w_ffn_out[l].astype(BF16),
        ln2_g=row(ln2_g), ln2_b=row(ln2_b),
    )


PROMPT_TT = 64
FFN_TM = 512
SAMPLE_NB = 16


def kernel(x_prompt, x_sample, cache_win_k, cache_win_v, state_conv, state_lru, meta_tokens, w_in, w_attn_proj, w_lru_proj, w_out, attn_sinks, conv_w, conv_b, lru_wa, lru_ba, lru_wx, lru_bx, lru_lambda, ln1_g, ln1_b, w_ffn_in, w_ffn_out, ln2_g, ln2_b):
    depth = w_in.shape[0]
    alpha = (2 * depth) ** 0.25
    batch, seq, d = x_prompt.shape
    dec_batch, dec_seq, _ = x_sample.shape

    xp = jnp.transpose(x_prompt, (1, 0, 2)).reshape(seq * batch, d)
    xm = jnp.broadcast_to(meta_tokens.astype(F32)[:, None, :], (N_META, batch, d)).reshape(N_META * batch, d)
    xs = jnp.transpose(x_sample, (1, 0, 2)).reshape(dec_seq * dec_batch, d)

    cos_m, sin_m = _rope_tables(jnp.arange(N_META, dtype=jnp.int32), batch)
    cos_p, sin_p = _rope_tables(N_META + jnp.arange(seq, dtype=jnp.int32), batch)
    cos_s, sin_s = _rope_tables(PAST_LEN + jnp.arange(dec_seq, dtype=jnp.int32), dec_batch)

    hist_p = (CONV_W - 1) * batch
    zero_state = (jnp.zeros((batch, WINDOW, KV_WIDTH), F32), jnp.zeros((batch, WINDOW, KV_WIDTH), F32),
                  jnp.zeros((hist_p, D_RNN), F32), jnp.zeros((batch, D_RNN), F32))

    outs = [[] for _ in range(8)]
    for l in range(depth):
        lw = _layer_weights(l, w_in, w_attn_proj, w_lru_proj, w_out, attn_sinks, conv_w, conv_b,
                            lru_wa, lru_ba, lru_wx, lru_bx, lru_lambda, ln1_g, ln1_b,
                            w_ffn_in, w_ffn_out, ln2_g, ln2_b)
        xm1, meta_state = _mixer_call(xm, cos_m, sin_m, lw, zero_state,
                                      ns=batch, tt=N_META, pos0=0, alpha=alpha)
        xm = _ffn_call(xm1, lw, tm=N_META * batch, alpha=alpha)
        xp1, (kw, vw, cv, hh) = _mixer_call(xp, cos_p, sin_p, lw, meta_state,
                                            ns=batch, tt=PROMPT_TT, pos0=N_META, alpha=alpha)
        xp = _ffn_call(xp1, lw, tm=FFN_TM, alpha=alpha)
        outs[0].append(kw.reshape(batch, WINDOW, N_KV_HEADS, HEAD_DIM))
        outs[1].append(vw.reshape(batch, WINDOW, N_KV_HEADS, HEAD_DIM))
        outs[2].append(jnp.transpose(cv.reshape(CONV_W - 1, batch, D_RNN), (1, 0, 2)))
        outs[3].append(hh)

        conv0 = jnp.transpose(state_conv[l], (1, 0, 2)).reshape((CONV_W - 1) * dec_batch, D_RNN)
        q, k, v, rec, cv_s, hh_s = _sample_pre_call(xs, cos_s, sin_s, lw, conv0, state_lru[l],
                                                    ns=dec_batch, tt=dec_seq)
        q5 = jnp.transpose(q.reshape(dec_seq, dec_batch, N_KV_HEADS, Q_PER_KV, HEAD_DIM), (1, 2, 3, 0, 4))
        q5 = q5.reshape(dec_batch, N_KV_HEADS, Q_PER_KV * dec_seq, HEAD_DIM)
        kn = jnp.transpose(k.reshape(dec_seq, dec_batch, KV_WIDTH), (1, 0, 2))
        vn = jnp.transpose(v.reshape(dec_seq, dec_batch, KV_WIDTH), (1, 0, 2))
        kc = cache_win_k[l].reshape(dec_batch, WINDOW, KV_WIDTH)
        vc = cache_win_v[l].reshape(dec_batch, WINDOW, KV_WIDTH)
        o5, ko, vo = _sample_attn_call(lw['sinks'], q5, kn, vn, kc, vc, nb=SAMPLE_NB, tt=dec_seq)
        attn = jnp.transpose(o5.reshape(dec_batch, N_KV_HEADS, Q_PER_KV, dec_seq, HEAD_DIM), (3, 0, 1, 2, 4))
        attn = attn.reshape(dec_seq * dec_batch, ATTN_WIDTH)
        xs1 = _sample_post_call(xs, attn, rec, lw, alpha=alpha)
        xs = _ffn_call(xs1, lw, tm=dec_seq * dec_batch, alpha=alpha)
        outs[4].append(ko.reshape(dec_batch, WINDOW, N_KV_HEADS, HEAD_DIM))
        outs[5].append(vo.reshape(dec_batch, WINDOW, N_KV_HEADS, HEAD_DIM))
        outs[6].append(jnp.transpose(cv_s.reshape(CONV_W - 1, dec_batch, D_RNN), (1, 0, 2)))
        outs[7].append(hh_s)

    y_prompt = jnp.transpose(xp.reshape(seq, batch, d), (1, 0, 2))
    y_sample = jnp.transpose(xs.reshape(dec_seq, dec_batch, d), (1, 0, 2))
    st = [jnp.stack(o) for o in outs]
    return (y_prompt, y_sample, st[0], st[1], st[2], st[3], st[4], st[5], st[6], st[7])
```

```python
import functools

import jax
import jax.numpy as jnp
from jax import lax
from jax.experimental import pallas as pl
from jax.experimental.pallas import tpu as pltpu

D_MODEL = 1024
N_META = 16
HEAD_DIM = 64
N_Q_HEADS = 8
N_KV_HEADS = 2
Q_PER_KV = N_Q_HEADS // N_KV_HEADS
ATTN_WIDTH = N_Q_HEADS * HEAD_DIM
KV_WIDTH = N_KV_HEADS * HEAD_DIM
WINDOW = 128
ROPE_THETA = 10000.0
D_RNN = D_MODEL
N_LRU_BLOCKS = 8
LRU_BLOCK = D_RNN // N_LRU_BLOCKS
CONV_W = 4
LRU_C = 8.0
D_FF = -(-8 * D_MODEL // (3 * 256)) * 256
LN_EPS = 1e-5
NEG_INF = -1e30
PAST_LEN = 8192

Q_END = ATTN_WIDTH
K_END = Q_END + KV_WIDTH
V_END = K_END + KV_WIDTH
XR_END = V_END + D_RNN
GATE_END = XR_END + D_RNN
D_IN = GATE_END + 2 * D_MODEL

LANES = 128
VMEM_LIMIT_BYTES = 56 * 1024 * 1024

PROMPT_TT = 64
FFN_TM = 512

F32 = jnp.float32
BF16 = jnp.bfloat16
_TRANS_B = (((1,), (1,)), ((), ()))


def _dot(a, b):
    return jnp.dot(a, b, preferred_element_type=F32)


def _dot_tb(a, b):
    return lax.dot_general(a, b, _TRANS_B, preferred_element_type=F32)


def _layer_norm(y, g, b):
    mu = jnp.mean(y, axis=-1, keepdims=True)
    d = y - mu
    var = jnp.mean(d * d, axis=-1, keepdims=True)
    return d * lax.rsqrt(var + LN_EPS) * g + b


def _rope(x, cos, sin_signed):
    lane = lax.broadcasted_iota(jnp.int32, (x.shape[0], LANES), 1)
    first_half = (lane & (HEAD_DIM - 1)) < (HEAD_DIM // 2)
    outs = []
    for g in range(x.shape[1] // LANES):
        xg = x[:, g * LANES:(g + 1) * LANES]
        partner = jnp.where(first_half,
                            pltpu.roll(xg, LANES - HEAD_DIM // 2, 1),
                            pltpu.roll(xg, HEAD_DIM // 2, 1))
        outs.append(xg * cos + partner * sin_signed)
    return outs[0] if len(outs) == 1 else jnp.concatenate(outs, axis=1)


def _softmax_pv(s, valid, sink_col, vv):
    s = jnp.where(valid, s, NEG_INF)
    m = jnp.maximum(jnp.max(s, axis=-1, keepdims=True), sink_col)
    p = jnp.exp(s - m)
    denom = jnp.sum(p, axis=-1, keepdims=True) + jnp.exp(sink_col - m)
    return _dot(p.astype(BF16), vv) * (1.0 / denom)


def _sink_column(sink_ref, layer, kv, tt):
    row = lax.broadcasted_iota(jnp.int32, (Q_PER_KV * tt, 1), 0)
    col = jnp.full((Q_PER_KV * tt, 1), sink_ref[layer, Q_PER_KV * kv + Q_PER_KV - 1], F32)
    for g in range(Q_PER_KV - 2, -1, -1):
        col = jnp.where(row < (g + 1) * tt, sink_ref[layer, Q_PER_KV * kv + g], col)
    return col


def _window_valid(tt, w, jmin):
    t_idx = lax.broadcasted_iota(jnp.int32, (Q_PER_KV * tt, w), 0) & (tt - 1)
    j_idx = lax.broadcasted_iota(jnp.int32, (Q_PER_KV * tt, w), 1)
    return (j_idx > t_idx) & (j_idx <= t_idx + WINDOW) & (j_idx >= jmin)


def _recurrent_branch(xr, gate, xe_ref, a_ref, b_ref, hs_ref, h_ref,
                      convw_ref, convb_ref, wax_ref, ba_ref, bx_ref, lam_ref, *, ns, tt):
    r_rows = tt * ns
    hist = (CONV_W - 1) * ns
    xe_ref[hist:hist + r_rows, :] = xr
    xc = convb_ref[...] + xe_ref[0:r_rows, :] * convw_ref[0:1, :]
    for j in range(1, CONV_W):
        xc = xc + xe_ref[j * ns:j * ns + r_rows, :] * convw_ref[j:j + 1, :]
    new_hist = xe_ref[r_rows:r_rows + hist, :]
    xe_ref[0:hist, :] = new_hist

    xcb = xc.astype(BF16)
    r_parts, i_parts = [], []
    for n in range(N_LRU_BLOCKS):
        ri = _dot(xcb[:, n * LRU_BLOCK:(n + 1) * LRU_BLOCK], wax_ref[n])
        r_parts.append(ri[:, :LRU_BLOCK])
        i_parts.append(ri[:, LRU_BLOCK:])
    r_gate = jax.nn.sigmoid(jnp.concatenate(r_parts, axis=1) + ba_ref[...])
    i_gate = jax.nn.sigmoid(jnp.concatenate(i_parts, axis=1) + bx_ref[...])
    log_a = (-LRU_C) * r_gate * jax.nn.softplus(-lam_ref[...])
    a = jnp.exp(log_a)
    a_ref[...] = a
    b_ref[...] = jnp.sqrt(-jnp.tanh(log_a) * (a * a + 1.0)) * (i_gate * xc)

    h = h_ref[...]
    for t in range(tt):
        h = a_ref[t * ns:(t + 1) * ns, :] * h + b_ref[t * ns:(t + 1) * ns, :]
        hs_ref[t * ns:(t + 1) * ns, :] = h
    h_ref[...] = h
    return hs_ref[...] * jax.nn.gelu(gate)


def _merge_norm(x, xb, attn, rec, win_ref, wap_ref, wlp_ref, wout_ref, g1_ref, b1_ref, alpha):
    gates = _dot(xb, win_ref[:, GATE_END:D_IN])
    ap = _dot(attn.astype(BF16), wap_ref[...])
    lp = _dot(rec.astype(BF16), wlp_ref[...])
    merged = (jax.nn.sigmoid(gates[:, :D_MODEL]) * ap
              + jax.nn.sigmoid(gates[:, D_MODEL:]) * lp)
    mix = _dot(merged.astype(BF16), wout_ref[...])
    return _layer_norm(alpha * x + mix, g1_ref[...], b1_ref[...])


def _layer_spec(arr, layer):
    nd = arr.ndim
    return pl.BlockSpec((None,) + tuple(arr.shape[1:]), lambda *_: (layer,) + (0,) * (nd - 1),
                        pipeline_mode=pl.Buffered(1))


def _full_spec(shape):
    nd = len(shape)
    return pl.BlockSpec(tuple(shape), lambda *_: (0,) * nd, pipeline_mode=pl.Buffered(1))


_COMPILER_PARAMS = pltpu.CompilerParams(dimension_semantics=("arbitrary",),
                                        vmem_limit_bytes=VMEM_LIMIT_BYTES)


def _mixer_kernel(sink_ref, x_ref, cos_ref, sin_ref,
                  win_ref, wap_ref, wlp_ref, wout_ref,
                  convw_ref, convb_ref, wax_ref, ba_ref, bx_ref, lam_ref, g1_ref, b1_ref,
                  kwin0_ref, vwin0_ref, conv0_ref, h0_ref,
                  y_ref, kwin_out_ref, vwin_out_ref, conv_out_ref, h_out_ref,
                  q_s, k_s, v_s, attn_s, kwin_s, vwin_s, xe_s, a_s, b_s, hs_s, h_s,
                  *, layer, ns, tt, pos0, alpha):
    step = pl.program_id(0)
    hist = (CONV_W - 1) * ns

    @pl.when(step == 0)
    def _():
        kwin_s[:, 0:WINDOW, :] = kwin0_ref[...]
        vwin_s[:, 0:WINDOW, :] = vwin0_ref[...]
        xe_s[0:hist, :] = conv0_ref[...]
        h_s[...] = h0_ref[...]

    xb = x_ref[...].astype(BF16)

    qkv = _dot(xb, win_ref[:, 0:V_END])
    cos = cos_ref[...]
    sin = sin_ref[...]
    n_qg = ATTN_WIDTH // LANES
    for j in range(n_qg):
        q_s[j] = _rope(qkv[:, j * LANES:(j + 1) * LANES], cos, sin)
    k_s[...] = _rope(qkv[:, Q_END:K_END], cos, sin)
    v_s[...] = qkv[:, K_END:V_END]

    jmin = WINDOW - (pos0 + step * tt)
    valid = _window_valid(tt, WINDOW + tt, jmin)
    sink_cols = [_sink_column(sink_ref, layer, kv, tt) for kv in range(N_KV_HEADS)]
    for b in range(ns):
        rows = pl.ds(b, tt, stride=ns)
        kwin_s[b, WINDOW:WINDOW + tt, :] = k_s[rows, :]
        vwin_s[b, WINDOW:WINDOW + tt, :] = v_s[rows, :]
        keys = kwin_s[b]
        vals = vwin_s[b]
        kb = keys.astype(BF16)
        vb = vals.astype(BF16)
        q_b = [q_s[j, rows, :].astype(BF16) for j in range(n_qg)]
        heads = [q_b[h // 2][:, (h % 2) * HEAD_DIM:(h % 2 + 1) * HEAD_DIM] for h in range(N_Q_HEADS)]
        outs = []
        for kv in range(N_KV_HEADS):
            kk = kb[:, kv * HEAD_DIM:(kv + 1) * HEAD_DIM]
            vv = vb[:, kv * HEAD_DIM:(kv + 1) * HEAD_DIM]
            qs = jnp.concatenate(heads[Q_PER_KV * kv:Q_PER_KV * (kv + 1)], axis=0)
            s = _dot_tb(qs, kk) * (HEAD_DIM ** -0.5)
            o = _softmax_pv(s, valid, sink_cols[kv], vv)
            outs.extend(o[g * tt:(g + 1) * tt] for g in range(Q_PER_KV))
        for j in range(n_qg):
            attn_s[j, rows, :] = jnp.concatenate(outs[2 * j:2 * j + 2], axis=1)
        kwin_s[b, 0:WINDOW, :] = keys[tt:tt + WINDOW]
        vwin_s[b, 0:WINDOW, :] = vals[tt:tt + WINDOW]

    xg = _dot(xb, win_ref[:, V_END:GATE_END])
    rec = _recurrent_branch(xg[:, :D_RNN], xg[:, D_RNN:], xe_s, a_s, b_s, hs_s, h_s,
                            convw_ref, convb_ref, wax_ref, ba_ref, bx_ref, lam_ref,
                            ns=ns, tt=tt)

    attn = jnp.concatenate([attn_s[j] for j in range(n_qg)], axis=1)
    y_ref[...] = _merge_norm(x_ref[...], xb, attn, rec, win_ref, wap_ref, wlp_ref, wout_ref,
                             g1_ref, b1_ref, alpha)

    @pl.when(step == pl.num_programs(0) - 1)
    def _():
        kwin_out_ref[...] = kwin_s[:, 0:WINDOW, :]
        vwin_out_ref[...] = vwin_s[:, 0:WINDOW, :]
        conv_out_ref[...] = xe_s[0:hist, :]
        h_out_ref[...] = h_s[...]


def _mixer_call(x, cos, sin, wts, state, *, layer, ns, tt, pos0, alpha):
    rows = x.shape[0]
    r_rows = ns * tt
    n_tiles = rows // r_rows
    hist = (CONV_W - 1) * ns
    kwin0, vwin0, conv0, h0 = state

    def row_spec(c):
        return pl.BlockSpec((r_rows, c), lambda i: (i, 0))

    weights = [wts[k] for k in ('w_in', 'w_attn_proj', 'w_lru_proj', 'w_out', 'conv_w', 'conv_b', 'w_ax',
                                'lru_ba', 'lru_bx', 'lru_lambda', 'ln1_g', 'ln1_b')]
    in_specs = ([pl.BlockSpec(memory_space=pltpu.SMEM), row_spec(D_MODEL), row_spec(LANES), row_spec(LANES)]
                + [_layer_spec(w, layer) for w in weights]
                + [_full_spec(s.shape) for s in (kwin0, vwin0, conv0, h0)])
    out_shape = (jax.ShapeDtypeStruct((rows, D_MODEL), F32),
                 jax.ShapeDtypeStruct((ns, WINDOW, KV_WIDTH), F32),
                 jax.ShapeDtypeStruct((ns, WINDOW, KV_WIDTH), F32),
                 jax.ShapeDtypeStruct((hist, D_RNN), F32),
                 jax.ShapeDtypeStruct((ns, D_RNN), F32))
    out_specs = (row_spec(D_MODEL),
                 pl.BlockSpec((ns, WINDOW, KV_WIDTH), lambda i: (0, 0, 0)),
                 pl.BlockSpec((ns, WINDOW, KV_WIDTH), lambda i: (0, 0, 0)),
                 pl.BlockSpec((hist, D_RNN), lambda i: (0, 0)),
                 pl.BlockSpec((ns, D_RNN), lambda i: (0, 0)))
    scratch = [pltpu.VMEM((ATTN_WIDTH // LANES, r_rows, LANES), F32),
               pltpu.VMEM((r_rows, KV_WIDTH), F32),
               pltpu.VMEM((r_rows, KV_WIDTH), F32),
               pltpu.VMEM((ATTN_WIDTH // LANES, r_rows, LANES), F32),
               pltpu.VMEM((ns, WINDOW + tt, KV_WIDTH), F32),
               pltpu.VMEM((ns, WINDOW + tt, KV_WIDTH), F32),
               pltpu.VMEM((hist + r_rows, D_RNN), F32),
               pltpu.VMEM((r_rows, D_RNN), F32),
               pltpu.VMEM((r_rows, D_RNN), F32),
               pltpu.VMEM((r_rows, D_RNN), F32),
               pltpu.VMEM((ns, D_RNN), F32)]
    outs = pl.pallas_call(
        functools.partial(_mixer_kernel, layer=layer, ns=ns, tt=tt, pos0=pos0, alpha=alpha),
        grid=(n_tiles,),
        in_specs=in_specs,
        out_specs=out_specs,
        out_shape=out_shape,
        scratch_shapes=scratch,
        compiler_params=_COMPILER_PARAMS,
        name=f"mixer_ns{ns}_tt{tt}",
    )(wts['sinks'], x, cos, sin, *weights, kwin0, vwin0, conv0, h0)
    return outs[0], outs[1:]


def _ffn_kernel(x_ref, win_ref, wout_ref, g_ref, b_ref, y_ref, *, alpha):
    x = x_ref[...]
    u = _dot(x.astype(BF16), win_ref[...])
    hmid = jax.nn.silu(u[:, :D_FF]) * u[:, D_FF:]
    f = _dot(hmid.astype(BF16), wout_ref[...])
    y_ref[...] = _layer_norm(alpha * x + f, g_ref[...], b_ref[...])


def _ffn_call(x, wts, *, layer, tm, alpha):
    rows = x.shape[0]
    weights = [wts[k] for k in ('w_ffn_in', 'w_ffn_out', 'ln2_g', 'ln2_b')]
    return pl.pallas_call(
        functools.partial(_ffn_kernel, alpha=alpha),
        grid=(rows // tm,),
        in_specs=[pl.BlockSpec((tm, D_MODEL), lambda i: (i, 0))] + [_layer_spec(w, layer) for w in weights],
        out_specs=pl.BlockSpec((tm, D_MODEL), lambda i: (i, 0)),
        out_shape=jax.ShapeDtypeStruct((rows, D_MODEL), F32),
        compiler_params=_COMPILER_PARAMS,
        name=f"ffn_tm{tm}",
    )(x, *weights)


def _sample_pre_kernel(x_ref, cos_ref, sin_ref, win_ref,
                       convw_ref, convb_ref, wax_ref, ba_ref, bx_ref, lam_ref,
                       conv0_ref, h0_ref,
                       q_ref, k_ref, v_ref, rec_ref, conv_out_ref, h_out_ref,
                       xe_s, a_s, b_s, hs_s, h_s, *, ns, tt):
    hist = (CONV_W - 1) * ns
    xe_s[0:hist, :] = conv0_ref[...]
    h_s[...] = h0_ref[...]
    xb = x_ref[...].astype(BF16)
    qkv = _dot(xb, win_ref[:, 0:V_END])
    cos = cos_ref[...]
    sin = sin_ref[...]
    q_ref[...] = _rope(qkv[:, :Q_END], cos, sin)
    k_ref[...] = _rope(qkv[:, Q_END:K_END], cos, sin)
    v_ref[...] = qkv[:, K_END:V_END]
    xg = _dot(xb, win_ref[:, V_END:GATE_END])
    rec_ref[...] = _recurrent_branch(xg[:, :D_RNN], xg[:, D_RNN:], xe_s, a_s, b_s, hs_s, h_s,
                                     convw_ref, convb_ref, wax_ref, ba_ref, bx_ref, lam_ref,
                                     ns=ns, tt=tt)
    conv_out_ref[...] = xe_s[0:hist, :]
    h_out_ref[...] = h_s[...]


def _sample_pre_call(x, cos, sin, wts, conv_all, h_all, *, layer, ns, tt):
    rows = ns * tt
    hist = (CONV_W - 1) * ns
    weights = [wts[k] for k in ('w_in', 'conv_w', 'conv_b', 'w_ax', 'lru_ba', 'lru_bx', 'lru_lambda')]
    out_shape = (jax.ShapeDtypeStruct((rows, ATTN_WIDTH), F32),
                 jax.ShapeDtypeStruct((rows, KV_WIDTH), F32),
                 jax.ShapeDtypeStruct((rows, KV_WIDTH), F32),
                 jax.ShapeDtypeStruct((rows, D_RNN), F32),
                 jax.ShapeDtypeStruct((hist, D_RNN), F32),
                 jax.ShapeDtypeStruct((ns, D_RNN), F32))
    return pl.pallas_call(
        functools.partial(_sample_pre_kernel, ns=ns, tt=tt),
        grid=(1,),
        in_specs=([_full_spec(a.shape) for a in (x, cos, sin)]
                  + [_layer_spec(w, layer) for w in weights]
                  + [_layer_spec(conv_all, layer), _layer_spec(h_all, layer)]),
        out_specs=tuple(pl.BlockSpec(s.shape, lambda i, _n=len(s.shape): (0,) * _n) for s in out_shape),
        out_shape=out_shape,
        scratch_shapes=[pltpu.VMEM((hist + rows, D_RNN), F32),
                        pltpu.VMEM((rows, D_RNN), F32),
                        pltpu.VMEM((rows, D_RNN), F32),
                        pltpu.VMEM((rows, D_RNN), F32),
                        pltpu.VMEM((ns, D_RNN), F32)],
        compiler_params=_COMPILER_PARAMS,
        name="sample_pre",
    )(x, cos, sin, *weights, conv_all, h_all)


def _sample_attn_kernel(sink_ref, q_ref, knt_ref, vnt_ref, kc_ref, vc_ref,
                        kprev_ref, vprev_ref, o_ref, ko_ref, vo_ref, *, layer, nb, tt):
    del kprev_ref, vprev_ref
    rows = N_Q_HEADS * tt
    scale = HEAD_DIM ** -0.5
    t_row = lax.broadcasted_iota(jnp.int32, (rows, 2 * WINDOW), 0) & (tt - 1)
    col = lax.broadcasted_iota(jnp.int32, (rows, 2 * WINDOW), 1)
    valid_old = (col < WINDOW) & (col > t_row) & (col >= WINDOW - PAST_LEN)
    lane = lax.broadcasted_iota(jnp.int32, (KV_WIDTH, WINDOW), 1)
    keep_old = lane < WINDOW - tt
    row1 = lax.broadcasted_iota(jnp.int32, (rows, 1), 0)
    sink_col = jnp.full((rows, 1), sink_ref[layer, N_Q_HEADS - 1], F32)
    for h in range(N_Q_HEADS - 2, -1, -1):
        sink_col = jnp.where(row1 < (h + 1) * tt, sink_ref[layer, h], sink_col)
    knt = knt_ref[...]
    vnt = vnt_ref[...]
    knt_b = knt.astype(BF16)
    vnt_b = vnt.astype(BF16)
    for s in range(nb):
        k_old = kc_ref[s]
        v_old = vc_ref[s]
        lo = WINDOW + s * tt
        valid = valid_old | ((col >= lo) & (col <= lo + t_row))
        k_all = jnp.concatenate([k_old.astype(BF16), knt_b], axis=1)
        v_all = jnp.concatenate([v_old.astype(BF16), vnt_b], axis=1)
        sc = jnp.where(valid, _dot(q_ref[s].astype(BF16), k_all) * scale, NEG_INF)
        m = jnp.maximum(jnp.max(sc, axis=-1, keepdims=True), sink_col)
        p = jnp.exp(sc - m)
        denom = jnp.sum(p, axis=-1, keepdims=True) + jnp.exp(sink_col - m)
        o_ref[s] = _dot_tb(p.astype(BF16), v_all) * (1.0 / denom)
        new_shift = (WINDOW - tt - tt * s) % WINDOW
        ko_ref[s] = jnp.where(keep_old, pltpu.roll(k_old, WINDOW - tt, 1), pltpu.roll(knt, new_shift, 1))
        vo_ref[s] = jnp.where(keep_old, pltpu.roll(v_old, WINDOW - tt, 1), pltpu.roll(vnt, new_shift, 1))


def _sample_attn_call(sinks, qz, knt, vnt, kc_all, vc_all, kprev, vprev, *, layer, tt):
    nseq = qz.shape[0]
    nb = WINDOW // tt

    q_spec = pl.BlockSpec((nb,) + tuple(qz.shape[1:]), lambda i: (i, 0, 0))
    cache_spec = pl.BlockSpec((None, nb, KV_WIDTH, WINDOW), lambda i: (layer, i, 0, 0))
    col_spec = pl.BlockSpec((KV_WIDTH, WINDOW), lambda i: (0, i))
    any_spec = pl.BlockSpec(memory_space=pl.ANY)
    out_shape = (jax.ShapeDtypeStruct(qz.shape, F32),
                 jax.ShapeDtypeStruct(kprev.shape, F32),
                 jax.ShapeDtypeStruct(vprev.shape, F32))
    return pl.pallas_call(
        functools.partial(_sample_attn_kernel, layer=layer, nb=nb, tt=tt),
        grid=(nseq // nb,),
        in_specs=[pl.BlockSpec(memory_space=pltpu.SMEM), q_spec, col_spec, col_spec,
                  cache_spec, cache_spec, any_spec, any_spec],
        out_specs=(q_spec, cache_spec, cache_spec),
        out_shape=out_shape,
        input_output_aliases={6: 1, 7: 2},
        compiler_params=_COMPILER_PARAMS,
        name="sample_attn",
    )(sinks, qz, knt, vnt, kc_all, vc_all, kprev, vprev)


def _sample_post_kernel(x_ref, attn_ref, rec_ref, win_ref, wap_ref, wlp_ref, wout_ref,
                        g1_ref, b1_ref, y_ref, *, alpha):
    x = x_ref[...]
    y_ref[...] = _merge_norm(x, x.astype(BF16), attn_ref[...], rec_ref[...],
                             win_ref, wap_ref, wlp_ref, wout_ref, g1_ref, b1_ref, alpha)


def _sample_post_call(x, attn, rec, wts, *, layer, alpha):
    weights = [wts[k] for k in ('w_in', 'w_attn_proj', 'w_lru_proj', 'w_out', 'ln1_g', 'ln1_b')]
    return pl.pallas_call(
        functools.partial(_sample_post_kernel, alpha=alpha),
        grid=(1,),
        in_specs=[_full_spec(a.shape) for a in (x, attn, rec)] + [_layer_spec(w, layer) for w in weights],
        out_specs=pl.BlockSpec(x.shape, lambda i: (0, 0)),
        out_shape=jax.ShapeDtypeStruct(x.shape, F32),
        compiler_params=_COMPILER_PARAMS,
        name="sample_post",
    )(x, attn, rec, *weights)


def _rope_tables(pos, repeat):
    half = HEAD_DIM // 2
    inv = ROPE_THETA ** (-jnp.arange(half, dtype=F32) / half)
    ang = pos.astype(F32)[:, None] * inv[None, :]
    cos = jnp.cos(ang)
    sin = jnp.sin(ang)
    cos_t = jnp.tile(cos, (1, LANES // half))
    sin_t = jnp.tile(jnp.concatenate([-sin, sin], axis=1), (1, LANES // HEAD_DIM))
    return jnp.repeat(cos_t, repeat, axis=0), jnp.repeat(sin_t, repeat, axis=0)


def _prepare_weights(w_in, w_attn_proj, w_lru_proj, w_out, attn_sinks, conv_w, conv_b,
                     lru_wa, lru_ba, lru_wx, lru_bx, lru_lambda, ln1_g, ln1_b,
                     w_ffn_in, w_ffn_out, ln2_g, ln2_b):
    depth = w_in.shape[0]
    row = lambda v: v.reshape(depth, 1, -1).astype(F32)
    return dict(
        w_in=w_in.astype(BF16),
        w_attn_proj=w_attn_proj.astype(BF16),
        w_lru_proj=w_lru_proj.astype(BF16),
        w_out=w_out.astype(BF16),
        sinks=attn_sinks.astype(F32),
        conv_w=conv_w.astype(F32),
        conv_b=row(conv_b),
        w_ax=jnp.concatenate([lru_wa, lru_wx], axis=-1).astype(BF16),
        lru_ba=row(lru_ba), lru_bx=row(lru_bx), lru_lambda=row(lru_lambda),
        ln1_g=row(ln1_g), ln1_b=row(ln1_b),
        w_ffn_in=w_ffn_in.astype(BF16),
        w_ffn_out=w_ffn_out.astype(BF16),
        ln2_g=row(ln2_g), ln2_b=row(ln2_b),
    )


def kernel(x_prompt, x_sample, cache_win_k, cache_win_v, state_conv, state_lru, meta_tokens, w_in, w_attn_proj, w_lru_proj, w_out, attn_sinks, conv_w, conv_b, lru_wa, lru_ba, lru_wx, lru_bx, lru_lambda, ln1_g, ln1_b, w_ffn_in, w_ffn_out, ln2_g, ln2_b):
    depth = w_in.shape[0]
    alpha = (2 * depth) ** 0.25
    batch, seq, d = x_prompt.shape
    dec_batch, dec_seq, _ = x_sample.shape
    wts = _prepare_weights(w_in, w_attn_proj, w_lru_proj, w_out, attn_sinks, conv_w, conv_b,
                           lru_wa, lru_ba, lru_wx, lru_bx, lru_lambda, ln1_g, ln1_b,
                           w_ffn_in, w_ffn_out, ln2_g, ln2_b)

    xp = jnp.transpose(x_prompt, (1, 0, 2)).reshape(seq * batch, d)
    xm = jnp.broadcast_to(meta_tokens.astype(F32)[:, None, :], (N_META, batch, d)).reshape(N_META * batch, d)
    xs = jnp.transpose(x_sample, (1, 0, 2)).reshape(dec_seq * dec_batch, d)

    cos_m, sin_m = _rope_tables(jnp.arange(N_META, dtype=jnp.int32), batch)
    cos_p, sin_p = _rope_tables(N_META + jnp.arange(seq, dtype=jnp.int32), batch)
    cos_s, sin_s = _rope_tables(PAST_LEN + jnp.arange(dec_seq, dtype=jnp.int32), dec_batch)

    hist_p = (CONV_W - 1) * batch
    zero_state = (jnp.zeros((batch, WINDOW, KV_WIDTH), F32), jnp.zeros((batch, WINDOW, KV_WIDTH), F32),
                  jnp.zeros((hist_p, D_RNN), F32), jnp.zeros((batch, D_RNN), F32))

    conv_all = jnp.transpose(state_conv, (0, 2, 1, 3)).reshape(depth, (CONV_W - 1) * dec_batch, D_RNN)
    kc_all = jnp.transpose(cache_win_k, (0, 1, 3, 4, 2)).reshape(depth, dec_batch, KV_WIDTH, WINDOW)
    vc_all = jnp.transpose(cache_win_v, (0, 1, 3, 4, 2)).reshape(depth, dec_batch, KV_WIDTH, WINDOW)
    win_k_s = jnp.zeros(kc_all.shape, F32)
    win_v_s = jnp.zeros(vc_all.shape, F32)

    outs = [[] for _ in range(6)]
    for l in range(depth):
        xm1, meta_state = _mixer_call(xm, cos_m, sin_m, wts, zero_state,
                                      layer=l, ns=batch, tt=N_META, pos0=0, alpha=alpha)
        xm = _ffn_call(xm1, wts, layer=l, tm=N_META * batch, alpha=alpha)
        xp1, (kw, vw, cv, hh) = _mixer_call(xp, cos_p, sin_p, wts, meta_state,
                                            layer=l, ns=batch, tt=PROMPT_TT, pos0=N_META, alpha=alpha)
        xp = _ffn_call(xp1, wts, layer=l, tm=FFN_TM, alpha=alpha)
        outs[0].append(kw.reshape(batch, WINDOW, N_KV_HEADS, HEAD_DIM))
        outs[1].append(vw.reshape(batch, WINDOW, N_KV_HEADS, HEAD_DIM))
        outs[2].append(jnp.transpose(cv.reshape(CONV_W - 1, batch, D_RNN), (1, 0, 2)))
        outs[3].append(hh)

        q, k, v, rec, cv_s, hh_s = _sample_pre_call(xs, cos_s, sin_s, wts, conv_all, state_lru,
                                                    layer=l, ns=dec_batch, tt=dec_seq)
        q5 = jnp.transpose(q.reshape(dec_seq, dec_batch, N_KV_HEADS, Q_PER_KV, HEAD_DIM), (1, 2, 3, 0, 4))
        own_group = jnp.eye(N_KV_HEADS, dtype=bool)[None, :, None, None, :, None]
        qz = jnp.where(own_group, q5[:, :, :, :, None, :], 0.0).reshape(dec_batch, N_Q_HEADS * dec_seq, KV_WIDTH)
        knt = jnp.transpose(k.reshape(dec_seq, dec_batch, KV_WIDTH), (2, 1, 0)).reshape(KV_WIDTH, dec_batch * dec_seq)
        vnt = jnp.transpose(v.reshape(dec_seq, dec_batch, KV_WIDTH), (2, 1, 0)).reshape(KV_WIDTH, dec_batch * dec_seq)
        oz, win_k_s, win_v_s = _sample_attn_call(wts['sinks'], qz, knt, vnt, kc_all, vc_all,
                                                 win_k_s, win_v_s, layer=l, tt=dec_seq)
        o6 = oz.reshape(dec_batch, N_KV_HEADS, Q_PER_KV, dec_seq, N_KV_HEADS, HEAD_DIM)
        o5 = jnp.stack([o6[:, kv, :, :, kv, :] for kv in range(N_KV_HEADS)], axis=1)
        attn = jnp.transpose(o5, (3, 0, 1, 2, 4)).reshape(dec_seq * dec_batch, ATTN_WIDTH)
        xs1 = _sample_post_call(xs, attn, rec, wts, layer=l, alpha=alpha)
        xs = _ffn_call(xs1, wts, layer=l, tm=dec_seq * dec_batch, alpha=alpha)
        outs[4].append(cv_s.reshape(CONV_W - 1, dec_batch, D_RNN))
        outs[5].append(hh_s)

    y_prompt = jnp.transpose(xp.reshape(seq, batch, d), (1, 0, 2))
    y_sample = jnp.transpose(xs.reshape(dec_seq, dec_batch, d), (1, 0, 2))
    st = [jnp.stack(o) for o in outs]
    win_shape = (depth, dec_batch, N_KV_HEADS, HEAD_DIM, WINDOW)
    win_k_sample = jnp.transpose(win_k_s.reshape(win_shape), (0, 1, 4, 2, 3))
    win_v_sample = jnp.transpose(win_v_s.reshape(win_shape), (0, 1, 4, 2, 3))
    conv_sample = jnp.transpose(st[4], (0, 2, 1, 3))
    return (y_prompt, y_sample, st[0], st[1], st[2], st[3], win_k_sample, win_v_sample, conv_sample, st[5])
```

```python
import functools

import jax
import jax.numpy as jnp
from jax import lax
from jax.experimental import pallas as pl
from jax.experimental.pallas import tpu as pltpu

D_MODEL = 1024
N_META = 16
HEAD_DIM = 64
N_Q_HEADS = 8
N_KV_HEADS = 2
Q_PER_KV = N_Q_HEADS // N_KV_HEADS
ATTN_WIDTH = N_Q_HEADS * HEAD_DIM
KV_WIDTH = N_KV_HEADS * HEAD_DIM
WINDOW = 128
ROPE_THETA = 10000.0
D_RNN = D_MODEL
N_LRU_BLOCKS = 8
LRU_BLOCK = D_RNN // N_LRU_BLOCKS
CONV_W = 4
LRU_C = 8.0
D_FF = -(-8 * D_MODEL // (3 * 256)) * 256
LN_EPS = 1e-5
NEG_INF = -1e30
PAST_LEN = 8192

Q_END = ATTN_WIDTH
K_END = Q_END + KV_WIDTH
V_END = K_END + KV_WIDTH
XR_END = V_END + D_RNN
GATE_END = XR_END + D_RNN
D_IN = GATE_END + 2 * D_MODEL

LANES = 128
VMEM_LIMIT_BYTES = 56 * 1024 * 1024

PROMPT_TT = 64
FFN_TM = 512

F32 = jnp.float32
BF16 = jnp.bfloat16
_TRANS_B = (((1,), (1,)), ((), ()))


def _dot(a, b):
    return jnp.dot(a, b, preferred_element_type=F32)


def _dot_tb(a, b):
    return lax.dot_general(a, b, _TRANS_B, preferred_element_type=F32)


def _layer_norm(y, g, b):
    mu = jnp.mean(y, axis=-1, keepdims=True)
    d = y - mu
    var = jnp.mean(d * d, axis=-1, keepdims=True)
    return d * lax.rsqrt(var + LN_EPS) * g + b


def _rope(x, cos, sin_signed):
    lane = lax.broadcasted_iota(jnp.int32, (x.shape[0], LANES), 1)
    first_half = (lane & (HEAD_DIM - 1)) < (HEAD_DIM // 2)
    outs = []
    for g in range(x.shape[1] // LANES):
        xg = x[:, g * LANES:(g + 1) * LANES]
        partner = jnp.where(first_half,
                            pltpu.roll(xg, LANES - HEAD_DIM // 2, 1),
                            pltpu.roll(xg, HEAD_DIM // 2, 1))
        outs.append(xg * cos + partner * sin_signed)
    return outs[0] if len(outs) == 1 else jnp.concatenate(outs, axis=1)


def _softmax_pv(s, valid, sink_col, vv):
    s = jnp.where(valid, s, NEG_INF)
    m = jnp.maximum(jnp.max(s, axis=-1, keepdims=True), sink_col)
    p = jnp.exp(s - m)
    denom = jnp.sum(p, axis=-1, keepdims=True) + jnp.exp(sink_col - m)
    return _dot(p.astype(BF16), vv) * (1.0 / denom)


def _sink_column(sink_ref, layer, kv, tt):
    row = lax.broadcasted_iota(jnp.int32, (Q_PER_KV * tt, 1), 0)
    col = jnp.full((Q_PER_KV * tt, 1), sink_ref[layer, Q_PER_KV * kv + Q_PER_KV - 1], F32)
    for g in range(Q_PER_KV - 2, -1, -1):
        col = jnp.where(row < (g + 1) * tt, sink_ref[layer, Q_PER_KV * kv + g], col)
    return col


def _window_valid(tt, w, jmin):
    t_idx = lax.broadcasted_iota(jnp.int32, (Q_PER_KV * tt, w), 0) & (tt - 1)
    j_idx = lax.broadcasted_iota(jnp.int32, (Q_PER_KV * tt, w), 1)
    return (j_idx > t_idx) & (j_idx <= t_idx + WINDOW) & (j_idx >= jmin)


def _recurrent_branch(xr, gate, xe_ref, a_ref, b_ref, hs_ref, h_ref,
                      convw_ref, convb_ref, wax_ref, ba_ref, bx_ref, lam_ref, *, ns, tt):
    r_rows = tt * ns
    hist = (CONV_W - 1) * ns
    xe_ref[hist:hist + r_rows, :] = xr
    xc = convb_ref[...] + xe_ref[0:r_rows, :] * convw_ref[0:1, :]
    for j in range(1, CONV_W):
        xc = xc + xe_ref[j * ns:j * ns + r_rows, :] * convw_ref[j:j + 1, :]
    new_hist = xe_ref[r_rows:r_rows + hist, :]
    xe_ref[0:hist, :] = new_hist

    xcb = xc.astype(BF16)
    r_parts, i_parts = [], []
    for n in range(N_LRU_BLOCKS):
        ri = _dot(xcb[:, n * LRU_BLOCK:(n + 1) * LRU_BLOCK], wax_ref[n])
        r_parts.append(ri[:, :LRU_BLOCK])
        i_parts.append(ri[:, LRU_BLOCK:])
    r_gate = jax.nn.sigmoid(jnp.concatenate(r_parts, axis=1) + ba_ref[...])
    i_gate = jax.nn.sigmoid(jnp.concatenate(i_parts, axis=1) + bx_ref[...])
    log_a = (-LRU_C) * r_gate * jax.nn.softplus(-lam_ref[...])
    a = jnp.exp(log_a)
    a_ref[...] = a
    b_ref[...] = jnp.sqrt(-jnp.tanh(log_a) * (a * a + 1.0)) * (i_gate * xc)

    h = h_ref[...]
    for t in range(tt):
        h = a_ref[t * ns:(t + 1) * ns, :] * h + b_ref[t * ns:(t + 1) * ns, :]
        hs_ref[t * ns:(t + 1) * ns, :] = h
    h_ref[...] = h
    return hs_ref[...] * jax.nn.gelu(gate)


def _merge_norm(x, gates, attn, rec, wap_ref, wlp_ref, wout_ref, g1_ref, b1_ref, alpha):
    ap = _dot(attn.astype(BF16), wap_ref[...])
    lp = _dot(rec.astype(BF16), wlp_ref[...])
    merged = (jax.nn.sigmoid(gates[:, :D_MODEL]) * ap
              + jax.nn.sigmoid(gates[:, D_MODEL:]) * lp)
    mix = _dot(merged.astype(BF16), wout_ref[...])
    return _layer_norm(alpha * x + mix, g1_ref[...], b1_ref[...])


def _layer_spec(arr, layer):
    nd = arr.ndim
    return pl.BlockSpec((None,) + tuple(arr.shape[1:]), lambda *_: (layer,) + (0,) * (nd - 1),
                        pipeline_mode=pl.Buffered(1))


def _full_spec(shape):
    nd = len(shape)
    return pl.BlockSpec(tuple(shape), lambda *_: (0,) * nd, pipeline_mode=pl.Buffered(1))


_COMPILER_PARAMS = pltpu.CompilerParams(dimension_semantics=("arbitrary",),
                                        vmem_limit_bytes=VMEM_LIMIT_BYTES)


def _mixer_kernel(sink_ref, x_ref, cos_ref, sin_ref,
                  win_ref, wap_ref, wlp_ref, wout_ref,
                  convw_ref, convb_ref, wax_ref, ba_ref, bx_ref, lam_ref, g1_ref, b1_ref,
                  kwin0_ref, vwin0_ref, conv0_ref, h0_ref,
                  y_ref, kwin_out_ref, vwin_out_ref, conv_out_ref, h_out_ref,
                  q_s, k_s, v_s, attn_s, kwin_s, vwin_s, xe_s, a_s, b_s, hs_s, h_s, *maybe_xt_s,
                  layer, ns, tt, pos0, alpha):
    step = pl.program_id(0)
    hist = (CONV_W - 1) * ns
    n_slabs = D_MODEL // LANES

    @pl.when(step == 0)
    def _():
        kwin_s[:, 0:WINDOW, :] = kwin0_ref[...]
        vwin_s[:, 0:WINDOW, :] = vwin0_ref[...]
        xe_s[0:hist, :] = conv0_ref[...]
        h_s[...] = h0_ref[...]

    if maybe_xt_s:
        xt_s, = maybe_xt_s
        for b in range(ns):
            for c in range(n_slabs):
                xt_s[c, pl.ds(b, tt, stride=ns), :] = x_ref[b, :, c * LANES:(c + 1) * LANES]
        load_x = lambda: jnp.concatenate([xt_s[c] for c in range(n_slabs)], axis=1)
    else:
        load_x = lambda: x_ref[...]

    proj = _dot(load_x().astype(BF16), win_ref[...])
    qkv = proj[:, 0:V_END]
    cos = cos_ref[...]
    sin = sin_ref[...]
    n_qg = ATTN_WIDTH // LANES
    for j in range(n_qg):
        q_s[j] = _rope(qkv[:, j * LANES:(j + 1) * LANES], cos, sin)
    k_s[...] = _rope(qkv[:, Q_END:K_END], cos, sin)
    v_s[...] = qkv[:, K_END:V_END]

    jmin = WINDOW - (pos0 + step * tt)
    valid = _window_valid(tt, WINDOW + tt, jmin)
    sink_cols = [_sink_column(sink_ref, layer, kv, tt) for kv in range(N_KV_HEADS)]
    for b in range(ns):
        rows = pl.ds(b, tt, stride=ns)
        kwin_s[b, WINDOW:WINDOW + tt, :] = k_s[rows, :]
        vwin_s[b, WINDOW:WINDOW + tt, :] = v_s[rows, :]
        keys = kwin_s[b]
        vals = vwin_s[b]
        kb = keys.astype(BF16)
        vb = vals.astype(BF16)
        q_b = [q_s[j, rows, :].astype(BF16) for j in range(n_qg)]
        heads = [q_b[h // 2][:, (h % 2) * HEAD_DIM:(h % 2 + 1) * HEAD_DIM] for h in range(N_Q_HEADS)]
        outs = []
        for kv in range(N_KV_HEADS):
            kk = kb[:, kv * HEAD_DIM:(kv + 1) * HEAD_DIM]
            vv = vb[:, kv * HEAD_DIM:(kv + 1) * HEAD_DIM]
            qs = jnp.concatenate(heads[Q_PER_KV * kv:Q_PER_KV * (kv + 1)], axis=0)
            s = _dot_tb(qs, kk) * (HEAD_DIM ** -0.5)
            o = _softmax_pv(s, valid, sink_cols[kv], vv)
            outs.extend(o[g * tt:(g + 1) * tt] for g in range(Q_PER_KV))
        for j in range(n_qg):
            attn_s[j, rows, :] = jnp.concatenate(outs[2 * j:2 * j + 2], axis=1)
        kwin_s[b, 0:WINDOW, :] = keys[tt:tt + WINDOW]
        vwin_s[b, 0:WINDOW, :] = vals[tt:tt + WINDOW]

    rec = _recurrent_branch(proj[:, V_END:XR_END], proj[:, XR_END:GATE_END], xe_s, a_s, b_s, hs_s, h_s,
                            convw_ref, convb_ref, wax_ref, ba_ref, bx_ref, lam_ref,
                            ns=ns, tt=tt)

    attn = jnp.concatenate([attn_s[j] for j in range(n_qg)], axis=1)
    y_ref[...] = _merge_norm(load_x(), proj[:, GATE_END:D_IN], attn, rec, wap_ref, wlp_ref, wout_ref,
                             g1_ref, b1_ref, alpha)

    @pl.when(step == pl.num_programs(0) - 1)
    def _():
        kwin_out_ref[...] = kwin_s[:, 0:WINDOW, :]
        vwin_out_ref[...] = vwin_s[:, 0:WINDOW, :]
        conv_out_ref[...] = xe_s[0:hist, :]
        h_out_ref[...] = h_s[...]


def _mixer_call(x, cos, sin, wts, state, *, layer, ns, tt, pos0, alpha):
    batch_major = x.ndim == 3
    rows = x.shape[0] * x.shape[1] if batch_major else x.shape[0]
    r_rows = ns * tt
    n_tiles = rows // r_rows
    hist = (CONV_W - 1) * ns
    kwin0, vwin0, conv0, h0 = state

    def row_spec(c):
        return pl.BlockSpec((r_rows, c), lambda i: (i, 0))

    x_spec = pl.BlockSpec((ns, tt, D_MODEL), lambda i: (0, i, 0)) if batch_major else row_spec(D_MODEL)
    weights = [wts[k] for k in ('w_in', 'w_attn_proj', 'w_lru_proj', 'w_out', 'conv_w', 'conv_b', 'w_ax',
                                'lru_ba', 'lru_bx', 'lru_lambda', 'ln1_g', 'ln1_b')]
    in_specs = ([pl.BlockSpec(memory_space=pltpu.SMEM), x_spec, row_spec(LANES), row_spec(LANES)]
                + [_layer_spec(w, layer) for w in weights]
                + [_full_spec(s.shape) for s in (kwin0, vwin0, conv0, h0)])
    out_shape = (jax.ShapeDtypeStruct((rows, D_MODEL), F32),
                 jax.ShapeDtypeStruct((ns, WINDOW, KV_WIDTH), F32),
                 jax.ShapeDtypeStruct((ns, WINDOW, KV_WIDTH), F32),
                 jax.ShapeDtypeStruct((hist, D_RNN), F32),
                 jax.ShapeDtypeStruct((ns, D_RNN), F32))
    out_specs = (row_spec(D_MODEL),
                 pl.BlockSpec((ns, WINDOW, KV_WIDTH), lambda i: (0, 0, 0)),
                 pl.BlockSpec((ns, WINDOW, KV_WIDTH), lambda i: (0, 0, 0)),
                 pl.BlockSpec((hist, D_RNN), lambda i: (0, 0)),
                 pl.BlockSpec((ns, D_RNN), lambda i: (0, 0)))
    scratch = [pltpu.VMEM((ATTN_WIDTH // LANES, r_rows, LANES), F32),
               pltpu.VMEM((r_rows, KV_WIDTH), F32),
               pltpu.VMEM((r_rows, KV_WIDTH), F32),
               pltpu.VMEM((ATTN_WIDTH // LANES, r_rows, LANES), F32),
               pltpu.VMEM((ns, WINDOW + tt, KV_WIDTH), F32),
               pltpu.VMEM((ns, WINDOW + tt, KV_WIDTH), F32),
               pltpu.VMEM((hist + r_rows, D_RNN), F32),
               pltpu.VMEM((r_rows, D_RNN), F32),
               pltpu.VMEM((r_rows, D_RNN), F32),
               pltpu.VMEM((r_rows, D_RNN), F32),
               pltpu.VMEM((ns, D_RNN), F32)]
    if batch_major:
        scratch.append(pltpu.VMEM((D_MODEL // LANES, r_rows, LANES), F32))
    outs = pl.pallas_call(
        functools.partial(_mixer_kernel, layer=layer, ns=ns, tt=tt, pos0=pos0, alpha=alpha),
        grid=(n_tiles,),
        in_specs=in_specs,
        out_specs=out_specs,
        out_shape=out_shape,
        scratch_shapes=scratch,
        compiler_params=_COMPILER_PARAMS,
        name=f"mixer_ns{ns}_tt{tt}",
    )(wts['sinks'], x, cos, sin, *weights, kwin0, vwin0, conv0, h0)
    return outs[0], outs[1:]


def _ffn_kernel(x_ref, win_ref, wout_ref, g_ref, b_ref, y_ref, *maybe_yt_s, alpha):
    x = x_ref[...]
    u = _dot(x.astype(BF16), win_ref[...])
    hmid = jax.nn.silu(u[:, :D_FF]) * u[:, D_FF:]
    f = _dot(hmid.astype(BF16), wout_ref[...])
    y = _layer_norm(alpha * x + f, g_ref[...], b_ref[...])
    if maybe_yt_s:
        yt_s, = maybe_yt_s
        ns, tt, _ = y_ref.shape
        for c in range(D_MODEL // LANES):
            yt_s[c] = y[:, c * LANES:(c + 1) * LANES]
        for b in range(ns):
            for c in range(D_MODEL // LANES):
                y_ref[b, :, c * LANES:(c + 1) * LANES] = yt_s[c, pl.ds(b, tt, stride=ns), :]
    else:
        y_ref[...] = y


def _ffn_call(x, wts, *, layer, tm, alpha, batch_major_out=None):
    rows = x.shape[0]
    weights = [wts[k] for k in ('w_ffn_in', 'w_ffn_out', 'ln2_g', 'ln2_b')]
    if batch_major_out is None:
        out_spec = pl.BlockSpec((tm, D_MODEL), lambda i: (i, 0))
        out_shape = jax.ShapeDtypeStruct((rows, D_MODEL), F32)
        scratch = []
    else:
        ns = batch_major_out
        out_spec = pl.BlockSpec((ns, tm // ns, D_MODEL), lambda i: (0, i, 0))
        out_shape = jax.ShapeDtypeStruct((ns, rows // ns, D_MODEL), F32)
        scratch = [pltpu.VMEM((D_MODEL // LANES, tm, LANES), F32)]
    return pl.pallas_call(
        functools.partial(_ffn_kernel, alpha=alpha),
        grid=(rows // tm,),
        in_specs=[pl.BlockSpec((tm, D_MODEL), lambda i: (i, 0))] + [_layer_spec(w, layer) for w in weights],
        out_specs=out_spec,
        out_shape=out_shape,
        scratch_shapes=scratch,
        compiler_params=_COMPILER_PARAMS,
        name=f"ffn_tm{tm}",
    )(x, *weights)


def _sample_pre_kernel(x_ref, cos_ref, sin_ref, win_ref,
                       convw_ref, convb_ref, wax_ref, ba_ref, bx_ref, lam_ref,
                       conv0_ref, h0_ref,
                       q_ref, k_ref, v_ref, rec_ref, conv_out_ref, h_out_ref,
                       xe_s, a_s, b_s, hs_s, h_s, *, ns, tt):
    hist = (CONV_W - 1) * ns
    xe_s[0:hist, :] = conv0_ref[...]
    h_s[...] = h0_ref[...]
    xb = x_ref[...].astype(BF16)
    qkv = _dot(xb, win_ref[:, 0:V_END])
    cos = cos_ref[...]
    sin = sin_ref[...]
    q_ref[...] = _rope(qkv[:, :Q_END], cos, sin)
    k_ref[...] = _rope(qkv[:, Q_END:K_END], cos, sin)
    v_ref[...] = qkv[:, K_END:V_END]
    xg = _dot(xb, win_ref[:, V_END:GATE_END])
    rec_ref[...] = _recurrent_branch(xg[:, :D_RNN], xg[:, D_RNN:], xe_s, a_s, b_s, hs_s, h_s,
                                     convw_ref, convb_ref, wax_ref, ba_ref, bx_ref, lam_ref,
                                     ns=ns, tt=tt)
    conv_out_ref[...] = xe_s[0:hist, :]
    h_out_ref[...] = h_s[...]


def _sample_pre_call(x, cos, sin, wts, conv_all, h_all, *, layer, ns, tt):
    rows = ns * tt
    hist = (CONV_W - 1) * ns
    weights = [wts[k] for k in ('w_in', 'conv_w', 'conv_b', 'w_ax', 'lru_ba', 'lru_bx', 'lru_lambda')]
    out_shape = (jax.ShapeDtypeStruct((rows, ATTN_WIDTH), F32),
                 jax.ShapeDtypeStruct((rows, KV_WIDTH), F32),
                 jax.ShapeDtypeStruct((rows, KV_WIDTH), F32),
                 jax.ShapeDtypeStruct((rows, D_RNN), F32),
                 jax.ShapeDtypeStruct((hist, D_RNN), F32),
                 jax.ShapeDtypeStruct((ns, D_RNN), F32))
    return pl.pallas_call(
        functools.partial(_sample_pre_kernel, ns=ns, tt=tt),
        grid=(1,),
        in_specs=([_full_spec(a.shape) for a in (x, cos, sin)]
                  + [_layer_spec(w, layer) for w in weights]
                  + [_layer_spec(conv_all, layer), _layer_spec(h_all, layer)]),
        out_specs=tuple(pl.BlockSpec(s.shape, lambda i, _n=len(s.shape): (0,) * _n) for s in out_shape),
        out_shape=out_shape,
        scratch_shapes=[pltpu.VMEM((hist + rows, D_RNN), F32),
                        pltpu.VMEM((rows, D_RNN), F32),
                        pltpu.VMEM((rows, D_RNN), F32),
                        pltpu.VMEM((rows, D_RNN), F32),
                        pltpu.VMEM((ns, D_RNN), F32)],
        compiler_params=_COMPILER_PARAMS,
        name="sample_pre",
    )(x, cos, sin, *weights, conv_all, h_all)


def _sample_attn_kernel(sink_ref, q_ref, knt_ref, vnt_ref, kc_ref, vc_ref,
                        kprev_ref, vprev_ref, o_ref, ko_ref, vo_ref, *, layer, nb, tt):
    del kprev_ref, vprev_ref
    rows = N_Q_HEADS * tt
    scale = HEAD_DIM ** -0.5
    t_row = lax.broadcasted_iota(jnp.int32, (rows, 2 * WINDOW), 0) & (tt - 1)
    col = lax.broadcasted_iota(jnp.int32, (rows, 2 * WINDOW), 1)
    valid_old = (col < WINDOW) & (col > t_row) & (col >= WINDOW - PAST_LEN)
    lane = lax.broadcasted_iota(jnp.int32, (KV_WIDTH, WINDOW), 1)
    keep_old = lane < WINDOW - tt
    row1 = lax.broadcasted_iota(jnp.int32, (rows, 1), 0)
    sink_col = jnp.full((rows, 1), sink_ref[layer, N_Q_HEADS - 1], F32)
    for h in range(N_Q_HEADS - 2, -1, -1):
        sink_col = jnp.where(row1 < (h + 1) * tt, sink_ref[layer, h], sink_col)
    knt = knt_ref[...]
    vnt = vnt_ref[...]
    knt_b = knt.astype(BF16)
    vnt_b = vnt.astype(BF16)
    for s in range(nb):
        k_old = kc_ref[s]
        v_old = vc_ref[s]
        lo = WINDOW + s * tt
        valid = valid_old | ((col >= lo) & (col <= lo + t_row))
        k_all = jnp.concatenate([k_old.astype(BF16), knt_b], axis=1)
        v_all = jnp.concatenate([v_old.astype(BF16), vnt_b], axis=1)
        sc = jnp.where(valid, _dot(q_ref[s].astype(BF16), k_all) * scale, NEG_INF)
        m = jnp.maximum(jnp.max(sc, axis=-1, keepdims=True), sink_col)
        p = jnp.exp(sc - m)
        denom = jnp.sum(p, axis=-1, keepdims=True) + jnp.exp(sink_col - m)
        o_ref[s] = _dot_tb(p.astype(BF16), v_all) * (1.0 / denom)
        new_shift = (WINDOW - tt - tt * s) % WINDOW
        ko_ref[s] = jnp.where(keep_old, pltpu.roll(k_old, WINDOW - tt, 1), pltpu.roll(knt, new_shift, 1))
        vo_ref[s] = jnp.where(keep_old, pltpu.roll(v_old, WINDOW - tt, 1), pltpu.roll(vnt, new_shift, 1))


def _sample_attn_call(sinks, qz, knt, vnt, kc_all, vc_all, kprev, vprev, *, layer, tt):
    nseq = qz.shape[0]
    nb = WINDOW // tt

    q_spec = pl.BlockSpec((nb,) + tuple(qz.shape[1:]), lambda i: (i, 0, 0))
    cache_spec = pl.BlockSpec((None, nb, KV_WIDTH, WINDOW), lambda i: (layer, i, 0, 0))
    col_spec = pl.BlockSpec((KV_WIDTH, WINDOW), lambda i: (0, i))
    any_spec = pl.BlockSpec(memory_space=pl.ANY)
    out_shape = (jax.ShapeDtypeStruct(qz.shape, F32),
                 jax.ShapeDtypeStruct(kprev.shape, F32),
                 jax.ShapeDtypeStruct(vprev.shape, F32))
    return pl.pallas_call(
        functools.partial(_sample_attn_kernel, layer=layer, nb=nb, tt=tt),
        grid=(nseq // nb,),
        in_specs=[pl.BlockSpec(memory_space=pltpu.SMEM), q_spec, col_spec, col_spec,
                  cache_spec, cache_spec, any_spec, any_spec],
        out_specs=(q_spec, cache_spec, cache_spec),
        out_shape=out_shape,
        input_output_aliases={6: 1, 7: 2},
        compiler_params=_COMPILER_PARAMS,
        name="sample_attn",
    )(sinks, qz, knt, vnt, kc_all, vc_all, kprev, vprev)


def _sample_post_kernel(x_ref, attn_ref, rec_ref, win_ref, wap_ref, wlp_ref, wout_ref,
                        g1_ref, b1_ref, y_ref, *, alpha):
    x = x_ref[...]
    gates = _dot(x.astype(BF16), win_ref[:, GATE_END:D_IN])
    y_ref[...] = _merge_norm(x, gates, attn_ref[...], rec_ref[...],
                             wap_ref, wlp_ref, wout_ref, g1_ref, b1_ref, alpha)


def _sample_post_call(x, attn, rec, wts, *, layer, alpha):
    weights = [wts[k] for k in ('w_in', 'w_attn_proj', 'w_lru_proj', 'w_out', 'ln1_g', 'ln1_b')]
    return pl.pallas_call(
        functools.partial(_sample_post_kernel, alpha=alpha),
        grid=(1,),
        in_specs=[_full_spec(a.shape) for a in (x, attn, rec)] + [_layer_spec(w, layer) for w in weights],
        out_specs=pl.BlockSpec(x.shape, lambda i: (0, 0)),
        out_shape=jax.ShapeDtypeStruct(x.shape, F32),
        compiler_params=_COMPILER_PARAMS,
        name="sample_post",
    )(x, attn, rec, *weights)


def _rope_tables(pos, repeat):
    half = HEAD_DIM // 2
    inv = ROPE_THETA ** (-jnp.arange(half, dtype=F32) / half)
    ang = pos.astype(F32)[:, None] * inv[None, :]
    cos = jnp.cos(ang)
    sin = jnp.sin(ang)
    cos_t = jnp.tile(cos, (1, LANES // half))
    sin_t = jnp.tile(jnp.concatenate([-sin, sin], axis=1), (1, LANES // HEAD_DIM))
    return jnp.repeat(cos_t, repeat, axis=0), jnp.repeat(sin_t, repeat, axis=0)


def _prepare_weights(w_in, w_attn_proj, w_lru_proj, w_out, attn_sinks, conv_w, conv_b,
                     lru_wa, lru_ba, lru_wx, lru_bx, lru_lambda, ln1_g, ln1_b,
                     w_ffn_in, w_ffn_out, ln2_g, ln2_b):
    depth = w_in.shape[0]
    row = lambda v: v.reshape(depth, 1, -1).astype(F32)
    return dict(
        w_in=w_in.astype(BF16),
        w_attn_proj=w_attn_proj.astype(BF16),
        w_lru_proj=w_lru_proj.astype(BF16),
        w_out=w_out.astype(BF16),
        sinks=attn_sinks.astype(F32),
        conv_w=conv_w.astype(F32),
        conv_b=row(conv_b),
        w_ax=jnp.concatenate([lru_wa, lru_wx], axis=-1).astype(BF16),
        lru_ba=row(lru_ba), lru_bx=row(lru_bx), lru_lambda=row(lru_lambda),
        ln1_g=row(ln1_g), ln1_b=row(ln1_b),
        w_ffn_in=w_ffn_in.astype(BF16),
        w_ffn_out=w_ffn_out.astype(BF16),
        ln2_g=row(ln2_g), ln2_b=row(ln2_b),
    )


def kernel(x_prompt, x_sample, cache_win_k, cache_win_v, state_conv, state_lru, meta_tokens, w_in, w_attn_proj, w_lru_proj, w_out, attn_sinks, conv_w, conv_b, lru_wa, lru_ba, lru_wx, lru_bx, lru_lambda, ln1_g, ln1_b, w_ffn_in, w_ffn_out, ln2_g, ln2_b):
    depth = w_in.shape[0]
    alpha = (2 * depth) ** 0.25
    batch, seq, d = x_prompt.shape
    dec_batch, dec_seq, _ = x_sample.shape
    wts = _prepare_weights(w_in, w_attn_proj, w_lru_proj, w_out, attn_sinks, conv_w, conv_b,
                           lru_wa, lru_ba, lru_wx, lru_bx, lru_lambda, ln1_g, ln1_b,
                           w_ffn_in, w_ffn_out, ln2_g, ln2_b)

    xp = x_prompt
    xm = jnp.broadcast_to(meta_tokens.astype(F32)[:, None, :], (N_META, batch, d)).reshape(N_META * batch, d)
    xs = jnp.transpose(x_sample, (1, 0, 2)).reshape(dec_seq * dec_batch, d)

    cos_m, sin_m = _rope_tables(jnp.arange(N_META, dtype=jnp.int32), batch)
    cos_p, sin_p = _rope_tables(N_META + jnp.arange(seq, dtype=jnp.int32), batch)
    cos_s, sin_s = _rope_tables(PAST_LEN + jnp.arange(dec_seq, dtype=jnp.int32), dec_batch)

    hist_p = (CONV_W - 1) * batch
    zero_state = (jnp.zeros((batch, WINDOW, KV_WIDTH), F32), jnp.zeros((batch, WINDOW, KV_WIDTH), F32),
                  jnp.zeros((hist_p, D_RNN), F32), jnp.zeros((batch, D_RNN), F32))

    conv_all = jnp.transpose(state_conv, (0, 2, 1, 3)).reshape(depth, (CONV_W - 1) * dec_batch, D_RNN)
    kc_all = jnp.transpose(cache_win_k, (0, 1, 3, 4, 2)).reshape(depth, dec_batch, KV_WIDTH, WINDOW)
    vc_all = jnp.transpose(cache_win_v, (0, 1, 3, 4, 2)).reshape(depth, dec_batch, KV_WIDTH, WINDOW)
    win_k_s = jnp.zeros(kc_all.shape, F32)
    win_v_s = jnp.zeros(vc_all.shape, F32)

    outs = [[] for _ in range(6)]
    for l in range(depth):
        xm1, meta_state = _mixer_call(xm, cos_m, sin_m, wts, zero_state,
                                      layer=l, ns=batch, tt=N_META, pos0=0, alpha=alpha)
        xm = _ffn_call(xm1, wts, layer=l, tm=N_META * batch, alpha=alpha)
        xp1, (kw, vw, cv, hh) = _mixer_call(xp, cos_p, sin_p, wts, meta_state,
                                            layer=l, ns=batch, tt=PROMPT_TT, pos0=N_META, alpha=alpha)
        xp = _ffn_call(xp1, wts, layer=l, tm=FFN_TM, alpha=alpha,
                       batch_major_out=batch if l == depth - 1 else None)
        outs[0].append(kw.reshape(batch, WINDOW, N_KV_HEADS, HEAD_DIM))
        outs[1].append(vw.reshape(batch, WINDOW, N_KV_HEADS, HEAD_DIM))
        outs[2].append(jnp.transpose(cv.reshape(CONV_W - 1, batch, D_RNN), (1, 0, 2)))
        outs[3].append(hh)

        q, k, v, rec, cv_s, hh_s = _sample_pre_call(xs, cos_s, sin_s, wts, conv_all, state_lru,
                                                    layer=l, ns=dec_batch, tt=dec_seq)
        q5 = jnp.transpose(q.reshape(dec_seq, dec_batch, N_KV_HEADS, Q_PER_KV, HEAD_DIM), (1, 2, 3, 0, 4))
        own_group = jnp.eye(N_KV_HEADS, dtype=bool)[None, :, None, None, :, None]
        qz = jnp.where(own_group, q5[:, :, :, :, None, :], 0.0).reshape(dec_batch, N_Q_HEADS * dec_seq, KV_WIDTH)
        knt = jnp.transpose(k.reshape(dec_seq, dec_batch, KV_WIDTH), (2, 1, 0)).reshape(KV_WIDTH, dec_batch * dec_seq)
        vnt = jnp.transpose(v.reshape(dec_seq, dec_batch, KV_WIDTH), (2, 1, 0)).reshape(KV_WIDTH, dec_batch * dec_seq)
        oz, win_k_s, win_v_s = _sample_attn_call(wts['sinks'], qz, knt, vnt, kc_all, vc_all,
                                                 win_k_s, win_v_s, layer=l, tt=dec_seq)
        o6 = oz.reshape(dec_batch, N_KV_HEADS, Q_PER_KV, dec_seq, N_KV_HEADS, HEAD_DIM)
        o5 = jnp.stack([o6[:, kv, :, :, kv, :] for kv in range(N_KV_HEADS)], axis=1)
        attn = jnp.transpose(o5, (3, 0, 1, 2, 4)).reshape(dec_seq * dec_batch, ATTN_WIDTH)
        xs1 = _sample_post_call(xs, attn, rec, wts, layer=l, alpha=alpha)
        xs = _ffn_call(xs1, wts, layer=l, tm=dec_seq * dec_batch, alpha=alpha)
        outs[4].append(cv_s.reshape(CONV_W - 1, dec_batch, D_RNN))
        outs[5].append(hh_s)

    y_prompt = xp
    y_sample = jnp.transpose(xs.reshape(dec_seq, dec_batch, d), (1, 0, 2))
    st = [jnp.stack(o) for o in outs]
    win_shape = (depth, dec_batch, N_KV_HEADS, HEAD_DIM, WINDOW)
    win_k_sample = jnp.transpose(win_k_s.reshape(win_shape), (0, 1, 4, 2, 3))
    win_v_sample = jnp.transpose(win_v_s.reshape(win_shape), (0, 1, 4, 2, 3))
    conv_sample = jnp.transpose(st[4], (0, 2, 1, 3))
    return (y_prompt, y_sample, st[0], st[1], st[2], st[3], win_k_sample, win_v_sample, conv_sample, st[5])
```

```python
import functools

import jax
import jax.numpy as jnp
from jax import lax
from jax.experimental import pallas as pl
from jax.experimental.pallas import tpu as pltpu

D_MODEL = 1024
N_META = 16
HEAD_DIM = 64
N_Q_HEADS = 8
N_KV_HEADS = 2
Q_PER_KV = N_Q_HEADS // N_KV_HEADS
ATTN_WIDTH = N_Q_HEADS * HEAD_DIM
KV_WIDTH = N_KV_HEADS * HEAD_DIM
WINDOW = 128
ROPE_THETA = 10000.0
D_RNN = D_MODEL
N_LRU_BLOCKS = 8
LRU_BLOCK = D_RNN // N_LRU_BLOCKS
CONV_W = 4
LRU_C = 8.0
D_FF = -(-8 * D_MODEL // (3 * 256)) * 256
LN_EPS = 1e-5
NEG_INF = -1e30
PAST_LEN = 8192

Q_END = ATTN_WIDTH
K_END = Q_END + KV_WIDTH
V_END = K_END + KV_WIDTH
XR_END = V_END + D_RNN
GATE_END = XR_END + D_RNN
D_IN = GATE_END + 2 * D_MODEL

LANES = 128
VMEM_LIMIT_BYTES = 56 * 1024 * 1024

PROMPT_TT = 64
FFN_TM = 512

F32 = jnp.float32
BF16 = jnp.bfloat16
_TRANS_B = (((1,), (1,)), ((), ()))


def _dot(a, b):
    return jnp.dot(a, b, preferred_element_type=F32)


def _dot_tb(a, b):
    return lax.dot_general(a, b, _TRANS_B, preferred_element_type=F32)


def _layer_norm(y, g, b):
    mu = jnp.mean(y, axis=-1, keepdims=True)
    d = y - mu
    var = jnp.mean(d * d, axis=-1, keepdims=True)
    return d * lax.rsqrt(var + LN_EPS) * g + b


def _rope(x, cos, sin_signed):
    lane = lax.broadcasted_iota(jnp.int32, (x.shape[0], LANES), 1)
    first_half = (lane & (HEAD_DIM - 1)) < (HEAD_DIM // 2)
    outs = []
    for g in range(x.shape[1] // LANES):
        xg = x[:, g * LANES:(g + 1) * LANES]
        partner = jnp.where(first_half,
                            pltpu.roll(xg, LANES - HEAD_DIM // 2, 1),
                            pltpu.roll(xg, HEAD_DIM // 2, 1))
        outs.append(xg * cos + partner * sin_signed)
    return outs[0] if len(outs) == 1 else jnp.concatenate(outs, axis=1)


def _softmax_pv(s, valid, sink_col, vv):
    s = jnp.where(valid, s, NEG_INF)
    m = jnp.maximum(jnp.max(s, axis=-1, keepdims=True), sink_col)
    p = jnp.exp(s - m)
    denom = jnp.sum(p, axis=-1, keepdims=True) + jnp.exp(sink_col - m)
    return _dot(p.astype(BF16), vv) * (1.0 / denom)


def _sink_column(sink_ref, layer, kv, tt):
    row = lax.broadcasted_iota(jnp.int32, (Q_PER_KV * tt, 1), 0)
    col = jnp.full((Q_PER_KV * tt, 1), sink_ref[layer, Q_PER_KV * kv + Q_PER_KV - 1], F32)
    for g in range(Q_PER_KV - 2, -1, -1):
        col = jnp.where(row < (g + 1) * tt, sink_ref[layer, Q_PER_KV * kv + g], col)
    return col


def _window_valid(tt, w, jmin):
    t_idx = lax.broadcasted_iota(jnp.int32, (Q_PER_KV * tt, w), 0) & (tt - 1)
    j_idx = lax.broadcasted_iota(jnp.int32, (Q_PER_KV * tt, w), 1)
    return (j_idx > t_idx) & (j_idx <= t_idx + WINDOW) & (j_idx >= jmin)


def _recurrent_branch(xr, gate, xe_ref, a_ref, b_ref, hs_ref, h_ref,
                      convw_ref, convb_ref, wax_ref, ba_ref, bx_ref, lam_ref, *, ns, tt):
    r_rows = tt * ns
    hist = (CONV_W - 1) * ns
    xe_ref[hist:hist + r_rows, :] = xr
    xc = convb_ref[...] + xe_ref[0:r_rows, :] * convw_ref[0:1, :]
    for j in range(1, CONV_W):
        xc = xc + xe_ref[j * ns:j * ns + r_rows, :] * convw_ref[j:j + 1, :]
    new_hist = xe_ref[r_rows:r_rows + hist, :]
    xe_ref[0:hist, :] = new_hist

    xcb = xc.astype(BF16)
    r_parts, i_parts = [], []
    for n in range(N_LRU_BLOCKS):
        ri = _dot(xcb[:, n * LRU_BLOCK:(n + 1) * LRU_BLOCK], wax_ref[n])
        r_parts.append(ri[:, :LRU_BLOCK])
        i_parts.append(ri[:, LRU_BLOCK:])
    r_gate = jax.nn.sigmoid(jnp.concatenate(r_parts, axis=1) + ba_ref[...])
    i_gate = jax.nn.sigmoid(jnp.concatenate(i_parts, axis=1) + bx_ref[...])
    log_a = (-LRU_C) * r_gate * jax.nn.softplus(-lam_ref[...])
    a = jnp.exp(log_a)
    a_ref[...] = a
    b_ref[...] = jnp.sqrt(-jnp.tanh(log_a) * (a * a + 1.0)) * (i_gate * xc)

    h = h_ref[...]
    for t in range(tt):
        h = a_ref[t * ns:(t + 1) * ns, :] * h + b_ref[t * ns:(t + 1) * ns, :]
        hs_ref[t * ns:(t + 1) * ns, :] = h
    h_ref[...] = h
    return hs_ref[...] * jax.nn.gelu(gate)


def _merge_norm(x, gates, attn, rec, wap_ref, wlp_ref, wout_ref, g1_ref, b1_ref, alpha):
    ap = _dot(attn.astype(BF16), wap_ref[...])
    lp = _dot(rec.astype(BF16), wlp_ref[...])
    merged = (jax.nn.sigmoid(gates[:, :D_MODEL]) * ap
              + jax.nn.sigmoid(gates[:, D_MODEL:]) * lp)
    mix = _dot(merged.astype(BF16), wout_ref[...])
    return _layer_norm(alpha * x + mix, g1_ref[...], b1_ref[...])


def _layer_spec(arr, layer):
    nd = arr.ndim
    return pl.BlockSpec((None,) + tuple(arr.shape[1:]), lambda *_: (layer,) + (0,) * (nd - 1),
                        pipeline_mode=pl.Buffered(1))


def _full_spec(shape):
    nd = len(shape)
    return pl.BlockSpec(tuple(shape), lambda *_: (0,) * nd, pipeline_mode=pl.Buffered(1))


_COMPILER_PARAMS = pltpu.CompilerParams(dimension_semantics=("arbitrary",),
                                        vmem_limit_bytes=VMEM_LIMIT_BYTES)


def _mixer_tile(load_x, cos, sin, y_ref, pos_start, sink_ref, win_ref, wap_ref, wlp_ref, wout_ref,
                convw_ref, convb_ref, wax_ref, ba_ref, bx_ref, lam_ref, g1_ref, b1_ref,
                q_s, k_s, v_s, attn_s, kwin_s, vwin_s, xe_s, a_s, b_s, hs_s, h_s, *, layer, ns, tt, alpha):
    r_rows = ns * tt
    n_qg = ATTN_WIDTH // LANES
    proj = _dot(load_x().astype(BF16), win_ref[...])
    qkv = proj[:, 0:V_END]
    for j in range(n_qg):
        q_s[j, 0:r_rows, :] = _rope(qkv[:, j * LANES:(j + 1) * LANES], cos, sin)
    k_s[0:r_rows, :] = _rope(qkv[:, Q_END:K_END], cos, sin)
    v_s[0:r_rows, :] = qkv[:, K_END:V_END]

    valid = _window_valid(tt, WINDOW + tt, WINDOW - pos_start)
    sink_cols = [_sink_column(sink_ref, layer, kv, tt) for kv in range(N_KV_HEADS)]
    for b in range(ns):
        rows = pl.ds(b, tt, stride=ns)
        kwin_s[b, WINDOW:WINDOW + tt, :] = k_s[rows, :]
        vwin_s[b, WINDOW:WINDOW + tt, :] = v_s[rows, :]
        keys = kwin_s[b, 0:WINDOW + tt, :]
        vals = vwin_s[b, 0:WINDOW + tt, :]
        kb = keys.astype(BF16)
        vb = vals.astype(BF16)
        q_b = [q_s[j, rows, :].astype(BF16) for j in range(n_qg)]
        heads = [q_b[h // 2][:, (h % 2) * HEAD_DIM:(h % 2 + 1) * HEAD_DIM] for h in range(N_Q_HEADS)]
        outs = []
        for kv in range(N_KV_HEADS):
            kk = kb[:, kv * HEAD_DIM:(kv + 1) * HEAD_DIM]
            vv = vb[:, kv * HEAD_DIM:(kv + 1) * HEAD_DIM]
            qs = jnp.concatenate(heads[Q_PER_KV * kv:Q_PER_KV * (kv + 1)], axis=0)
            s = _dot_tb(qs, kk) * (HEAD_DIM ** -0.5)
            o = _softmax_pv(s, valid, sink_cols[kv], vv)
            outs.extend(o[g * tt:(g + 1) * tt] for g in range(Q_PER_KV))
        for j in range(n_qg):
            attn_s[j, rows, :] = jnp.concatenate(outs[2 * j:2 * j + 2], axis=1)
        kwin_s[b, 0:WINDOW, :] = keys[tt:tt + WINDOW]
        vwin_s[b, 0:WINDOW, :] = vals[tt:tt + WINDOW]

    tile_rows = pl.ds(0, r_rows)
    rec = _recurrent_branch(proj[:, V_END:XR_END], proj[:, XR_END:GATE_END], xe_s,
                            a_s.at[tile_rows], b_s.at[tile_rows], hs_s.at[tile_rows], h_s,
                            convw_ref, convb_ref, wax_ref, ba_ref, bx_ref, lam_ref, ns=ns, tt=tt)
    attn = jnp.concatenate([attn_s[j, 0:r_rows, :] for j in range(n_qg)], axis=1)
    y_ref[...] = _merge_norm(load_x(), proj[:, GATE_END:D_IN], attn, rec, wap_ref, wlp_ref, wout_ref,
                             g1_ref, b1_ref, alpha)


def _mixer_kernel(sink_ref, xm_ref, x_ref, cosm_ref, sinm_ref, cos_ref, sin_ref,
                  win_ref, wap_ref, wlp_ref, wout_ref,
                  convw_ref, convb_ref, wax_ref, ba_ref, bx_ref, lam_ref, g1_ref, b1_ref,
                  ym_ref, y_ref, kwin_out_ref, vwin_out_ref, conv_out_ref, h_out_ref,
                  q_s, k_s, v_s, attn_s, kwin_s, vwin_s, xe_s, a_s, b_s, hs_s, h_s, *maybe_xt_s,
                  layer, ns, tt, tt_meta, alpha):
    step = pl.program_id(0)
    hist = (CONV_W - 1) * ns
    n_slabs = D_MODEL // LANES
    shared = (sink_ref, win_ref, wap_ref, wlp_ref, wout_ref,
              convw_ref, convb_ref, wax_ref, ba_ref, bx_ref, lam_ref, g1_ref, b1_ref,
              q_s, k_s, v_s, attn_s, kwin_s, vwin_s, xe_s, a_s, b_s, hs_s, h_s)

    @pl.when(step == 0)
    def _():
        kwin_s[:, 0:WINDOW, :] = jnp.zeros((ns, WINDOW, KV_WIDTH), F32)
        vwin_s[:, 0:WINDOW, :] = jnp.zeros((ns, WINDOW, KV_WIDTH), F32)
        xe_s[0:hist, :] = jnp.zeros((hist, D_RNN), F32)
        h_s[...] = jnp.zeros((ns, D_RNN), F32)
        _mixer_tile(lambda: xm_ref[...], cosm_ref[...], sinm_ref[...], ym_ref, 0, *shared,
                    layer=layer, ns=ns, tt=tt_meta, alpha=alpha)

    @pl.when(step > 0)
    def _():
        if maybe_xt_s:
            xt_s, = maybe_xt_s
            for b in range(ns):
                for c in range(n_slabs):
                    xt_s[c, pl.ds(b, tt, stride=ns), :] = x_ref[b, :, c * LANES:(c + 1) * LANES]
            load_x = lambda: jnp.concatenate([xt_s[c] for c in range(n_slabs)], axis=1)
        else:
            load_x = lambda: x_ref[...]
        _mixer_tile(load_x, cos_ref[...], sin_ref[...], y_ref, tt_meta + (step - 1) * tt, *shared,
                    layer=layer, ns=ns, tt=tt, alpha=alpha)

    @pl.when(step == pl.num_programs(0) - 1)
    def _():
        kwin_out_ref[...] = kwin_s[:, 0:WINDOW, :]
        vwin_out_ref[...] = vwin_s[:, 0:WINDOW, :]
        conv_out_ref[...] = xe_s[0:hist, :]
        h_out_ref[...] = h_s[...]


def _mixer_call(xm, x, tables_m, tables_p, wts, *, layer, ns, tt, alpha):
    batch_major = x.ndim == 3
    rows = x.shape[0] * x.shape[1] if batch_major else x.shape[0]
    r_rows = ns * tt
    n_tiles = rows // r_rows
    rows_m = xm.shape[0]
    tt_meta = rows_m // ns
    hist = (CONV_W - 1) * ns

    def row_spec(c):
        return pl.BlockSpec((r_rows, c), lambda i: (jnp.maximum(i - 1, 0), 0))

    if batch_major:
        x_spec = pl.BlockSpec((ns, tt, D_MODEL), lambda i: (0, jnp.maximum(i - 1, 0), 0))
    else:
        x_spec = row_spec(D_MODEL)
    weights = [wts[k] for k in ('w_in', 'w_attn_proj', 'w_lru_proj', 'w_out', 'conv_w', 'conv_b', 'w_ax',
                                'lru_ba', 'lru_bx', 'lru_lambda', 'ln1_g', 'ln1_b')]
    in_specs = ([pl.BlockSpec(memory_space=pltpu.SMEM), _full_spec(xm.shape), x_spec,
                 _full_spec(tables_m[0].shape), _full_spec(tables_m[1].shape), row_spec(LANES), row_spec(LANES)]
                + [_layer_spec(w, layer) for w in weights])
    out_shape = (jax.ShapeDtypeStruct((rows_m, D_MODEL), F32),
                 jax.ShapeDtypeStruct((rows, D_MODEL), F32),
                 jax.ShapeDtypeStruct((ns, WINDOW, KV_WIDTH), F32),
                 jax.ShapeDtypeStruct((ns, WINDOW, KV_WIDTH), F32),
                 jax.ShapeDtypeStruct((hist, D_RNN), F32),
                 jax.ShapeDtypeStruct((ns, D_RNN), F32))
    out_specs = (pl.BlockSpec((rows_m, D_MODEL), lambda i: (0, 0)),
                 row_spec(D_MODEL),
                 pl.BlockSpec((ns, WINDOW, KV_WIDTH), lambda i: (0, 0, 0)),
                 pl.BlockSpec((ns, WINDOW, KV_WIDTH), lambda i: (0, 0, 0)),
                 pl.BlockSpec((hist, D_RNN), lambda i: (0, 0)),
                 pl.BlockSpec((ns, D_RNN), lambda i: (0, 0)))
    scratch = [pltpu.VMEM((ATTN_WIDTH // LANES, r_rows, LANES), F32),
               pltpu.VMEM((r_rows, KV_WIDTH), F32),
               pltpu.VMEM((r_rows, KV_WIDTH), F32),
               pltpu.VMEM((ATTN_WIDTH // LANES, r_rows, LANES), F32),
               pltpu.VMEM((ns, WINDOW + tt, KV_WIDTH), F32),
               pltpu.VMEM((ns, WINDOW + tt, KV_WIDTH), F32),
               pltpu.VMEM((hist + r_rows, D_RNN), F32),
               pltpu.VMEM((r_rows, D_RNN), F32),
               pltpu.VMEM((r_rows, D_RNN), F32),
               pltpu.VMEM((r_rows, D_RNN), F32),
               pltpu.VMEM((ns, D_RNN), F32)]
    if batch_major:
        scratch.append(pltpu.VMEM((D_MODEL // LANES, r_rows, LANES), F32))
    outs = pl.pallas_call(
        functools.partial(_mixer_kernel, layer=layer, ns=ns, tt=tt, tt_meta=tt_meta, alpha=alpha),
        grid=(n_tiles + 1,),
        in_specs=in_specs,
        out_specs=out_specs,
        out_shape=out_shape,
        scratch_shapes=scratch,
        compiler_params=_COMPILER_PARAMS,
        name=f"mixer_ns{ns}_tt{tt}",
    )(wts['sinks'], xm, x, *tables_m, *tables_p, *weights)
    return outs[0], outs[1], outs[2:]


def _ffn_tile(x, win_ref, wout_ref, g_ref, b_ref, alpha):
    u = _dot(x.astype(BF16), win_ref[...])
    hmid = jax.nn.silu(u[:, :D_FF]) * u[:, D_FF:]
    f = _dot(hmid.astype(BF16), wout_ref[...])
    return _layer_norm(alpha * x + f, g_ref[...], b_ref[...])


def _ffn_kernel(x_ref, xe_ref, win_ref, wout_ref, g_ref, b_ref, y_ref, ye_ref, *maybe_yt_s, alpha):
    step = pl.program_id(0)
    n_main = pl.num_programs(0) - 1

    @pl.when(step < n_main)
    def _():
        y = _ffn_tile(x_ref[...], win_ref, wout_ref, g_ref, b_ref, alpha)
        if maybe_yt_s:
            yt_s, = maybe_yt_s
            ns, tt, _ = y_ref.shape
            for c in range(D_MODEL // LANES):
                yt_s[c] = y[:, c * LANES:(c + 1) * LANES]
            for b in range(ns):
                for c in range(D_MODEL // LANES):
                    y_ref[b, :, c * LANES:(c + 1) * LANES] = yt_s[c, pl.ds(b, tt, stride=ns), :]
        else:
            y_ref[...] = y

    @pl.when(step == n_main)
    def _():
        ye_ref[...] = _ffn_tile(xe_ref[...], win_ref, wout_ref, g_ref, b_ref, alpha)


def _ffn_call(x, x_extra, wts, *, layer, tm, alpha, batch_major_out=None):
    rows = x.shape[0]
    n_main = rows // tm
    weights = [wts[k] for k in ('w_ffn_in', 'w_ffn_out', 'ln2_g', 'ln2_b')]
    main_idx = lambda i: jnp.minimum(i, n_main - 1)
    if batch_major_out is None:
        out_spec = pl.BlockSpec((tm, D_MODEL), lambda i: (main_idx(i), 0))
        out_shape = jax.ShapeDtypeStruct((rows, D_MODEL), F32)
        scratch = []
    else:
        ns = batch_major_out
        out_spec = pl.BlockSpec((ns, tm // ns, D_MODEL), lambda i: (0, main_idx(i), 0))
        out_shape = jax.ShapeDtypeStruct((ns, rows // ns, D_MODEL), F32)
        scratch = [pltpu.VMEM((D_MODEL // LANES, tm, LANES), F32)]
    return pl.pallas_call(
        functools.partial(_ffn_kernel, alpha=alpha),
        grid=(n_main + 1,),
        in_specs=([pl.BlockSpec((tm, D_MODEL), lambda i: (main_idx(i), 0)), _full_spec(x_extra.shape)]
                  + [_layer_spec(w, layer) for w in weights]),
        out_specs=(out_spec, pl.BlockSpec(x_extra.shape, lambda i: (0, 0))),
        out_shape=(out_shape, jax.ShapeDtypeStruct(x_extra.shape, F32)),
        scratch_shapes=scratch,
        compiler_params=_COMPILER_PARAMS,
        name=f"ffn_tm{tm}",
    )(x, x_extra, *weights)


def _sample_pre_kernel(x_ref, cos_ref, sin_ref, win_ref,
                       convw_ref, convb_ref, wax_ref, ba_ref, bx_ref, lam_ref,
                       conv0_ref, h0_ref,
                       q_ref, k_ref, v_ref, rec_ref, conv_out_ref, h_out_ref,
                       xe_s, a_s, b_s, hs_s, h_s, *, ns, tt):
    hist = (CONV_W - 1) * ns
    xe_s[0:hist, :] = conv0_ref[...]
    h_s[...] = h0_ref[...]
    xb = x_ref[...].astype(BF16)
    qkv = _dot(xb, win_ref[:, 0:V_END])
    cos = cos_ref[...]
    sin = sin_ref[...]
    q_ref[...] = _rope(qkv[:, :Q_END], cos, sin)
    k_ref[...] = _rope(qkv[:, Q_END:K_END], cos, sin)
    v_ref[...] = qkv[:, K_END:V_END]
    xg = _dot(xb, win_ref[:, V_END:GATE_END])
    rec_ref[...] = _recurrent_branch(xg[:, :D_RNN], xg[:, D_RNN:], xe_s, a_s, b_s, hs_s, h_s,
                                     convw_ref, convb_ref, wax_ref, ba_ref, bx_ref, lam_ref,
                                     ns=ns, tt=tt)
    conv_out_ref[...] = xe_s[0:hist, :]
    h_out_ref[...] = h_s[...]


def _sample_pre_call(x, cos, sin, wts, conv_all, h_all, *, layer, ns, tt):
    rows = ns * tt
    hist = (CONV_W - 1) * ns
    weights = [wts[k] for k in ('w_in', 'conv_w', 'conv_b', 'w_ax', 'lru_ba', 'lru_bx', 'lru_lambda')]
    out_shape = (jax.ShapeDtypeStruct((rows, ATTN_WIDTH), F32),
                 jax.ShapeDtypeStruct((rows, KV_WIDTH), F32),
                 jax.ShapeDtypeStruct((rows, KV_WIDTH), F32),
                 jax.ShapeDtypeStruct((rows, D_RNN), F32),
                 jax.ShapeDtypeStruct((hist, D_RNN), F32),
                 jax.ShapeDtypeStruct((ns, D_RNN), F32))
    return pl.pallas_call(
        functools.partial(_sample_pre_kernel, ns=ns, tt=tt),
        grid=(1,),
        in_specs=([_full_spec(a.shape) for a in (x, cos, sin)]
                  + [_layer_spec(w, layer) for w in weights]
                  + [_layer_spec(conv_all, layer), _layer_spec(h_all, layer)]),
        out_specs=tuple(pl.BlockSpec(s.shape, lambda i, _n=len(s.shape): (0,) * _n) for s in out_shape),
        out_shape=out_shape,
        scratch_shapes=[pltpu.VMEM((hist + rows, D_RNN), F32),
                        pltpu.VMEM((rows, D_RNN), F32),
                        pltpu.VMEM((rows, D_RNN), F32),
                        pltpu.VMEM((rows, D_RNN), F32),
                        pltpu.VMEM((ns, D_RNN), F32)],
        compiler_params=_COMPILER_PARAMS,
        name="sample_pre",
    )(x, cos, sin, *weights, conv_all, h_all)


def _sample_attn_kernel(sink_ref, q_ref, knt_ref, vnt_ref, kc_ref, vc_ref,
                        kprev_ref, vprev_ref, o_ref, ko_ref, vo_ref, *, layer, nb, tt):
    del kprev_ref, vprev_ref
    rows = N_Q_HEADS * tt
    scale = HEAD_DIM ** -0.5
    t_row = lax.broadcasted_iota(jnp.int32, (rows, 2 * WINDOW), 0) & (tt - 1)
    col = lax.broadcasted_iota(jnp.int32, (rows, 2 * WINDOW), 1)
    valid_old = (col < WINDOW) & (col > t_row) & (col >= WINDOW - PAST_LEN)
    lane = lax.broadcasted_iota(jnp.int32, (KV_WIDTH, WINDOW), 1)
    keep_old = lane < WINDOW - tt
    row1 = lax.broadcasted_iota(jnp.int32, (rows, 1), 0)
    sink_col = jnp.full((rows, 1), sink_ref[layer, N_Q_HEADS - 1], F32)
    for h in range(N_Q_HEADS - 2, -1, -1):
        sink_col = jnp.where(row1 < (h + 1) * tt, sink_ref[layer, h], sink_col)
    knt = knt_ref[...]
    vnt = vnt_ref[...]
    knt_b = knt.astype(BF16)
    vnt_b = vnt.astype(BF16)
    for s in range(nb):
        k_old = kc_ref[s]
        v_old = vc_ref[s]
        lo = WINDOW + s * tt
        valid = valid_old | ((col >= lo) & (col <= lo + t_row))
        k_all = jnp.concatenate([k_old.astype(BF16), knt_b], axis=1)
        v_all = jnp.concatenate([v_old.astype(BF16), vnt_b], axis=1)
        sc = jnp.where(valid, _dot(q_ref[s].astype(BF16), k_all) * scale, NEG_INF)
        m = jnp.maximum(jnp.max(sc, axis=-1, keepdims=True), sink_col)
        p = jnp.exp(sc - m)
        denom = jnp.sum(p, axis=-1, keepdims=True) + jnp.exp(sink_col - m)
        o_ref[s] = _dot_tb(p.astype(BF16), v_all) * (1.0 / denom)
        new_shift = (WINDOW - tt - tt * s) % WINDOW
        ko_ref[s] = jnp.where(keep_old, pltpu.roll(k_old, WINDOW - tt, 1), pltpu.roll(knt, new_shift, 1))
        vo_ref[s] = jnp.where(keep_old, pltpu.roll(v_old, WINDOW - tt, 1), pltpu.roll(vnt, new_shift, 1))


def _sample_attn_call(sinks, qz, knt, vnt, kc_all, vc_all, kprev, vprev, *, layer, tt):
    nseq = qz.shape[0]
    nb = WINDOW // tt

    q_spec = pl.BlockSpec((nb,) + tuple(qz.shape[1:]), lambda i: (i, 0, 0))
    cache_spec = pl.BlockSpec((None, nb, KV_WIDTH, WINDOW), lambda i: (layer, i, 0, 0))
    col_spec = pl.BlockSpec((KV_WIDTH, WINDOW), lambda i: (0, i))
    any_spec = pl.BlockSpec(memory_space=pl.ANY)
    out_shape = (jax.ShapeDtypeStruct(qz.shape, F32),
                 jax.ShapeDtypeStruct(kprev.shape, F32),
                 jax.ShapeDtypeStruct(vprev.shape, F32))
    return pl.pallas_call(
        functools.partial(_sample_attn_kernel, layer=layer, nb=nb, tt=tt),
        grid=(nseq // nb,),
        in_specs=[pl.BlockSpec(memory_space=pltpu.SMEM), q_spec, col_spec, col_spec,
                  cache_spec, cache_spec, any_spec, any_spec],
        out_specs=(q_spec, cache_spec, cache_spec),
        out_shape=out_shape,
        input_output_aliases={6: 1, 7: 2},
        compiler_params=_COMPILER_PARAMS,
        name="sample_attn",
    )(sinks, qz, knt, vnt, kc_all, vc_all, kprev, vprev)


def _sample_post_kernel(x_ref, attn_ref, rec_ref, win_ref, wap_ref, wlp_ref, wout_ref,
                        g1_ref, b1_ref, y_ref, *, alpha):
    x = x_ref[...]
    gates = _dot(x.astype(BF16), win_ref[:, GATE_END:D_IN])
    y_ref[...] = _merge_norm(x, gates, attn_ref[...], rec_ref[...],
                             wap_ref, wlp_ref, wout_ref, g1_ref, b1_ref, alpha)


def _sample_post_call(x, attn, rec, wts, *, layer, alpha):
    weights = [wts[k] for k in ('w_in', 'w_attn_proj', 'w_lru_proj', 'w_out', 'ln1_g', 'ln1_b')]
    return pl.pallas_call(
        functools.partial(_sample_post_kernel, alpha=alpha),
        grid=(1,),
        in_specs=[_full_spec(a.shape) for a in (x, attn, rec)] + [_layer_spec(w, layer) for w in weights],
        out_specs=pl.BlockSpec(x.shape, lambda i: (0, 0)),
        out_shape=jax.ShapeDtypeStruct(x.shape, F32),
        compiler_params=_COMPILER_PARAMS,
        name="sample_post",
    )(x, attn, rec, *weights)


def _rope_tables(pos, repeat):
    half = HEAD_DIM // 2
    inv = ROPE_THETA ** (-jnp.arange(half, dtype=F32) / half)
    ang = pos.astype(F32)[:, None] * inv[None, :]
    cos = jnp.cos(ang)
    sin = jnp.sin(ang)
    cos_t = jnp.tile(cos, (1, LANES // half))
    sin_t = jnp.tile(jnp.concatenate([-sin, sin], axis=1), (1, LANES // HEAD_DIM))
    return jnp.repeat(cos_t, repeat, axis=0), jnp.repeat(sin_t, repeat, axis=0)


def _prepare_weights(w_in, w_attn_proj, w_lru_proj, w_out, attn_sinks, conv_w, conv_b,
                     lru_wa, lru_ba, lru_wx, lru_bx, lru_lambda, ln1_g, ln1_b,
                     w_ffn_in, w_ffn_out, ln2_g, ln2_b):
    depth = w_in.shape[0]
    row = lambda v: v.reshape(depth, 1, -1).astype(F32)
    return dict(
        w_in=w_in.astype(BF16),
        w_attn_proj=w_attn_proj.astype(BF16),
        w_lru_proj=w_lru_proj.astype(BF16),
        w_out=w_out.astype(BF16),
        sinks=attn_sinks.astype(F32),
        conv_w=conv_w.astype(F32),
        conv_b=row(conv_b),
        w_ax=jnp.concatenate([lru_wa, lru_wx], axis=-1).astype(BF16),
        lru_ba=row(lru_ba), lru_bx=row(lru_bx), lru_lambda=row(lru_lambda),
        ln1_g=row(ln1_g), ln1_b=row(ln1_b),
        w_ffn_in=w_ffn_in.astype(BF16),
        w_ffn_out=w_ffn_out.astype(BF16),
        ln2_g=row(ln2_g), ln2_b=row(ln2_b),
    )


def kernel(x_prompt, x_sample, cache_win_k, cache_win_v, state_conv, state_lru, meta_tokens, w_in, w_attn_proj, w_lru_proj, w_out, attn_sinks, conv_w, conv_b, lru_wa, lru_ba, lru_wx, lru_bx, lru_lambda, ln1_g, ln1_b, w_ffn_in, w_ffn_out, ln2_g, ln2_b):
    depth = w_in.shape[0]
    alpha = (2 * depth) ** 0.25
    batch, seq, d = x_prompt.shape
    dec_batch, dec_seq, _ = x_sample.shape
    wts = _prepare_weights(w_in, w_attn_proj, w_lru_proj, w_out, attn_sinks, conv_w, conv_b,
                           lru_wa, lru_ba, lru_wx, lru_bx, lru_lambda, ln1_g, ln1_b,
                           w_ffn_in, w_ffn_out, ln2_g, ln2_b)

    xp = x_prompt
    xm = jnp.broadcast_to(meta_tokens.astype(F32)[:, None, :], (N_META, batch, d)).reshape(N_META * batch, d)
    xs = jnp.transpose(x_sample, (1, 0, 2)).reshape(dec_seq * dec_batch, d)

    cos_m, sin_m = _rope_tables(jnp.arange(N_META, dtype=jnp.int32), batch)
    cos_p, sin_p = _rope_tables(N_META + jnp.arange(seq, dtype=jnp.int32), batch)
    cos_s, sin_s = _rope_tables(PAST_LEN + jnp.arange(dec_seq, dtype=jnp.int32), dec_batch)

    conv_all = jnp.transpose(state_conv, (0, 2, 1, 3)).reshape(depth, (CONV_W - 1) * dec_batch, D_RNN)
    kc_all = jnp.transpose(cache_win_k, (0, 1, 3, 4, 2)).reshape(depth, dec_batch, KV_WIDTH, WINDOW)
    vc_all = jnp.transpose(cache_win_v, (0, 1, 3, 4, 2)).reshape(depth, dec_batch, KV_WIDTH, WINDOW)
    win_k_s = jnp.zeros(kc_all.shape, F32)
    win_v_s = jnp.zeros(vc_all.shape, F32)

    outs = [[] for _ in range(6)]
    for l in range(depth):
        xm1, xp1, (kw, vw, cv, hh) = _mixer_call(xm, xp, (cos_m, sin_m), (cos_p, sin_p), wts,
                                                  layer=l, ns=batch, tt=PROMPT_TT, alpha=alpha)
        outs[0].append(kw.reshape(batch, WINDOW, N_KV_HEADS, HEAD_DIM))
        outs[1].append(vw.reshape(batch, WINDOW, N_KV_HEADS, HEAD_DIM))
        outs[2].append(jnp.transpose(cv.reshape(CONV_W - 1, batch, D_RNN), (1, 0, 2)))
        outs[3].append(hh)

        q, k, v, rec, cv_s, hh_s = _sample_pre_call(xs, cos_s, sin_s, wts, conv_all, state_lru,
                                                    layer=l, ns=dec_batch, tt=dec_seq)
        q5 = jnp.transpose(q.reshape(dec_seq, dec_batch, N_KV_HEADS, Q_PER_KV, HEAD_DIM), (1, 2, 3, 0, 4))
        own_group = jnp.eye(N_KV_HEADS, dtype=bool)[None, :, None, None, :, None]
        qz = jnp.where(own_group, q5[:, :, :, :, None, :], 0.0).reshape(dec_batch, N_Q_HEADS * dec_seq, KV_WIDTH)
        knt = jnp.transpose(k.reshape(dec_seq, dec_batch, KV_WIDTH), (2, 1, 0)).reshape(KV_WIDTH, dec_batch * dec_seq)
        vnt = jnp.transpose(v.reshape(dec_seq, dec_batch, KV_WIDTH), (2, 1, 0)).reshape(KV_WIDTH, dec_batch * dec_seq)
        oz, win_k_s, win_v_s = _sample_attn_call(wts['sinks'], qz, knt, vnt, kc_all, vc_all,
                                                 win_k_s, win_v_s, layer=l, tt=dec_seq)
        o6 = oz.reshape(dec_batch, N_KV_HEADS, Q_PER_KV, dec_seq, N_KV_HEADS, HEAD_DIM)
        o5 = jnp.stack([o6[:, kv, :, :, kv, :] for kv in range(N_KV_HEADS)], axis=1)
        attn = jnp.transpose(o5, (3, 0, 1, 2, 4)).reshape(dec_seq * dec_batch, ATTN_WIDTH)
        xs1 = _sample_post_call(xs, attn, rec, wts, layer=l, alpha=alpha)
        xp, xsm = _ffn_call(xp1, jnp.concatenate([xs1, xm1], axis=0), wts, layer=l, tm=FFN_TM, alpha=alpha,
                            batch_major_out=batch if l == depth - 1 else None)
        xs, xm = xsm[:dec_seq * dec_batch], xsm[dec_seq * dec_batch:]
        outs[4].append(cv_s.reshape(CONV_W - 1, dec_batch, D_RNN))
        outs[5].append(hh_s)

    y_prompt = xp
    y_sample = jnp.transpose(xs.reshape(dec_seq, dec_batch, d), (1, 0, 2))
    st = [jnp.stack(o) for o in outs]
    win_shape = (depth, dec_batch, N_KV_HEADS, HEAD_DIM, WINDOW)
    win_k_sample = jnp.transpose(win_k_s.reshape(win_shape), (0, 1, 4, 2, 3))
    win_v_sample = jnp.transpose(win_v_s.reshape(win_shape), (0, 1, 4, 2, 3))
    conv_sample = jnp.transpose(st[4], (0, 2, 1, 3))
    return (y_prompt, y_sample, st[0], st[1], st[2], st[3], win_k_sample, win_v_sample, conv_sample, st[5])
```

```python
import functools

import jax
import jax.numpy as jnp
from jax import lax
from jax.experimental import pallas as pl
from jax.experimental.pallas import tpu as pltpu

D_MODEL = 1024
N_META = 16
HEAD_DIM = 64
N_Q_HEADS = 8
N_KV_HEADS = 2
Q_PER_KV = N_Q_HEADS // N_KV_HEADS
ATTN_WIDTH = N_Q_HEADS * HEAD_DIM
KV_WIDTH = N_KV_HEADS * HEAD_DIM
WINDOW = 128
ROPE_THETA = 10000.0
D_RNN = D_MODEL
N_LRU_BLOCKS = 8
LRU_BLOCK = D_RNN // N_LRU_BLOCKS
CONV_W = 4
LRU_C = 8.0
D_FF = -(-8 * D_MODEL // (3 * 256)) * 256
LN_EPS = 1e-5
NEG_INF = -1e30
PAST_LEN = 8192

Q_END = ATTN_WIDTH
K_END = Q_END + KV_WIDTH
V_END = K_END + KV_WIDTH
XR_END = V_END + D_RNN
GATE_END = XR_END + D_RNN
D_IN = GATE_END + 2 * D_MODEL

LANES = 128
VMEM_LIMIT_BYTES = 56 * 1024 * 1024

PROMPT_TT = 64
FFN_TM = 512
FFN_CAST_CHUNKS = 16

F32 = jnp.float32
BF16 = jnp.bfloat16
_TRANS_B = (((1,), (1,)), ((), ()))


def _dot(a, b):
    return jnp.dot(a, b, preferred_element_type=F32)


def _dot_tb(a, b):
    return lax.dot_general(a, b, _TRANS_B, preferred_element_type=F32)


def _layer_norm(y, g, b):
    mu = jnp.mean(y, axis=-1, keepdims=True)
    d = y - mu
    var = jnp.mean(d * d, axis=-1, keepdims=True)
    return d * lax.rsqrt(var + LN_EPS) * g + b


def _rope(x, cos, sin_signed):
    lane = lax.broadcasted_iota(jnp.int32, (x.shape[0], LANES), 1)
    first_half = (lane & (HEAD_DIM - 1)) < (HEAD_DIM // 2)
    outs = []
    for g in range(x.shape[1] // LANES):
        xg = x[:, g * LANES:(g + 1) * LANES]
        partner = jnp.where(first_half,
                            pltpu.roll(xg, LANES - HEAD_DIM // 2, 1),
                            pltpu.roll(xg, HEAD_DIM // 2, 1))
        outs.append(xg * cos + partner * sin_signed)
    return outs[0] if len(outs) == 1 else jnp.concatenate(outs, axis=1)


def _softmax_pv(s, valid, sink_col, vv):
    s = jnp.where(valid, s, NEG_INF)
    m = jnp.maximum(jnp.max(s, axis=-1, keepdims=True), sink_col)
    p = jnp.exp(s - m)
    denom = jnp.sum(p, axis=-1, keepdims=True) + jnp.exp(sink_col - m)
    return _dot(p.astype(BF16), vv) * (1.0 / denom)


def _sink_column(sink_ref, layer, kv, tt):
    row = lax.broadcasted_iota(jnp.int32, (Q_PER_KV * tt, 1), 0)
    col = jnp.full((Q_PER_KV * tt, 1), sink_ref[layer, Q_PER_KV * kv + Q_PER_KV - 1], F32)
    for g in range(Q_PER_KV - 2, -1, -1):
        col = jnp.where(row < (g + 1) * tt, sink_ref[layer, Q_PER_KV * kv + g], col)
    return col


def _window_valid(tt, w, jmin):
    t_idx = lax.broadcasted_iota(jnp.int32, (Q_PER_KV * tt, w), 0) & (tt - 1)
    j_idx = lax.broadcasted_iota(jnp.int32, (Q_PER_KV * tt, w), 1)
    return (j_idx > t_idx) & (j_idx <= t_idx + WINDOW) & (j_idx >= jmin)


def _recurrent_branch(xr, gate, xe_ref, a_ref, b_ref, hs_ref, h_ref,
                      convw_ref, convb_ref, wax_ref, ba_ref, bx_ref, lam_ref, *, ns, tt):
    r_rows = tt * ns
    hist = (CONV_W - 1) * ns
    xe_ref[hist:hist + r_rows, :] = xr
    xc = convb_ref[...] + xe_ref[0:r_rows, :] * convw_ref[0:1, :]
    for j in range(1, CONV_W):
        xc = xc + xe_ref[j * ns:j * ns + r_rows, :] * convw_ref[j:j + 1, :]
    new_hist = xe_ref[r_rows:r_rows + hist, :]
    xe_ref[0:hist, :] = new_hist

    xcb = xc.astype(BF16)
    r_parts, i_parts = [], []
    for n in range(N_LRU_BLOCKS):
        ri = _dot(xcb[:, n * LRU_BLOCK:(n + 1) * LRU_BLOCK], wax_ref[n])
        r_parts.append(ri[:, :LRU_BLOCK])
        i_parts.append(ri[:, LRU_BLOCK:])
    r_gate = jax.nn.sigmoid(jnp.concatenate(r_parts, axis=1) + ba_ref[...])
    i_gate = jax.nn.sigmoid(jnp.concatenate(i_parts, axis=1) + bx_ref[...])
    log_a = (-LRU_C) * r_gate * jax.nn.softplus(-lam_ref[...])
    a = jnp.exp(log_a)
    a_ref[...] = a
    b_ref[...] = jnp.sqrt(-jnp.tanh(log_a) * (a * a + 1.0)) * (i_gate * xc)

    h = h_ref[...]
    for t in range(tt):
        h = a_ref[t * ns:(t + 1) * ns, :] * h + b_ref[t * ns:(t + 1) * ns, :]
        hs_ref[t * ns:(t + 1) * ns, :] = h
    h_ref[...] = h
    return hs_ref[...] * jax.nn.gelu(gate)


def _merge_norm(x, gates, attn, rec, wap_ref, wlp_ref, wout_ref, g1_ref, b1_ref, alpha):
    ap = _dot(attn.astype(BF16), wap_ref[...])
    lp = _dot(rec.astype(BF16), wlp_ref[...])
    merged = (jax.nn.sigmoid(gates[:, :D_MODEL]) * ap
              + jax.nn.sigmoid(gates[:, D_MODEL:]) * lp)
    mix = _dot(merged.astype(BF16), wout_ref[...])
    return _layer_norm(alpha * x + mix, g1_ref[...], b1_ref[...])


def _layer_spec(arr, layer):
    nd = arr.ndim
    return pl.BlockSpec((None,) + tuple(arr.shape[1:]), lambda *_: (layer,) + (0,) * (nd - 1),
                        pipeline_mode=pl.Buffered(1))


def _full_spec(shape):
    nd = len(shape)
    return pl.BlockSpec(tuple(shape), lambda *_: (0,) * nd, pipeline_mode=pl.Buffered(1))


_COMPILER_PARAMS = pltpu.CompilerParams(dimension_semantics=("arbitrary",),
                                        vmem_limit_bytes=VMEM_LIMIT_BYTES)


def _mixer_tile(load_x, cos, sin, y_ref, pos_start, sink_ref, win_ref, wap_ref, wlp_ref, wout_ref,
                convw_ref, convb_ref, wax_ref, ba_ref, bx_ref, lam_ref, g1_ref, b1_ref,
                q_s, k_s, v_s, attn_s, kwin_s, vwin_s, xe_s, a_s, b_s, hs_s, h_s, *, layer, ns, tt, alpha):
    r_rows = ns * tt
    n_qg = ATTN_WIDTH // LANES
    proj = _dot(load_x().astype(BF16), win_ref[...])
    qkv = proj[:, 0:V_END]
    for j in range(n_qg):
        q_s[j, 0:r_rows, :] = _rope(qkv[:, j * LANES:(j + 1) * LANES], cos, sin)
    k_s[0:r_rows, :] = _rope(qkv[:, Q_END:K_END], cos, sin)
    v_s[0:r_rows, :] = qkv[:, K_END:V_END]

    valid = _window_valid(tt, WINDOW + tt, WINDOW - pos_start)
    sink_cols = [_sink_column(sink_ref, layer, kv, tt) for kv in range(N_KV_HEADS)]
    for b in range(ns):
        rows = pl.ds(b, tt, stride=ns)
        kwin_s[b, WINDOW:WINDOW + tt, :] = k_s[rows, :]
        vwin_s[b, WINDOW:WINDOW + tt, :] = v_s[rows, :]
        keys = kwin_s[b, 0:WINDOW + tt, :]
        vals = vwin_s[b, 0:WINDOW + tt, :]
        kb = keys.astype(BF16)
        vb = vals.astype(BF16)
        q_b = [q_s[j, rows, :].astype(BF16) for j in range(n_qg)]
        heads = [q_b[h // 2][:, (h % 2) * HEAD_DIM:(h % 2 + 1) * HEAD_DIM] for h in range(N_Q_HEADS)]
        outs = []
        for kv in range(N_KV_HEADS):
            kk = kb[:, kv * HEAD_DIM:(kv + 1) * HEAD_DIM]
            vv = vb[:, kv * HEAD_DIM:(kv + 1) * HEAD_DIM]
            qs = jnp.concatenate(heads[Q_PER_KV * kv:Q_PER_KV * (kv + 1)], axis=0)
            s = _dot_tb(qs, kk) * (HEAD_DIM ** -0.5)
            o = _softmax_pv(s, valid, sink_cols[kv], vv)
            outs.extend(o[g * tt:(g + 1) * tt] for g in range(Q_PER_KV))
        for j in range(n_qg):
            attn_s[j, rows, :] = jnp.concatenate(outs[2 * j:2 * j + 2], axis=1)
        kwin_s[b, 0:WINDOW, :] = keys[tt:tt + WINDOW]
        vwin_s[b, 0:WINDOW, :] = vals[tt:tt + WINDOW]

    tile_rows = pl.ds(0, r_rows)
    rec = _recurrent_branch(proj[:, V_END:XR_END], proj[:, XR_END:GATE_END], xe_s,
                            a_s.at[tile_rows], b_s.at[tile_rows], hs_s.at[tile_rows], h_s,
                            convw_ref, convb_ref, wax_ref, ba_ref, bx_ref, lam_ref, ns=ns, tt=tt)
    attn = jnp.concatenate([attn_s[j, 0:r_rows, :] for j in range(n_qg)], axis=1)
    y_ref[...] = _merge_norm(load_x(), proj[:, GATE_END:D_IN], attn, rec, wap_ref, wlp_ref, wout_ref,
                             g1_ref, b1_ref, alpha)


def _mixer_kernel(sink_ref, xm_ref, x_ref, cosm_ref, sinm_ref, cos_ref, sin_ref,
                  win_ref, wap_ref, wlp_ref, wout_ref,
                  convw_ref, convb_ref, wax_ref, ba_ref, bx_ref, lam_ref, g1_ref, b1_ref,
                  ym_ref, y_ref, kwin_out_ref, vwin_out_ref, conv_out_ref, h_out_ref,
                  q_s, k_s, v_s, attn_s, kwin_s, vwin_s, xe_s, a_s, b_s, hs_s, h_s, *maybe_xt_s,
                  layer, ns, tt, tt_meta, alpha):
    step = pl.program_id(0)
    hist = (CONV_W - 1) * ns
    n_slabs = D_MODEL // LANES
    shared = (sink_ref, win_ref, wap_ref, wlp_ref, wout_ref,
              convw_ref, convb_ref, wax_ref, ba_ref, bx_ref, lam_ref, g1_ref, b1_ref,
              q_s, k_s, v_s, attn_s, kwin_s, vwin_s, xe_s, a_s, b_s, hs_s, h_s)

    @pl.when(step == 0)
    def _():
        kwin_s[:, 0:WINDOW, :] = jnp.zeros((ns, WINDOW, KV_WIDTH), F32)
        vwin_s[:, 0:WINDOW, :] = jnp.zeros((ns, WINDOW, KV_WIDTH), F32)
        xe_s[0:hist, :] = jnp.zeros((hist, D_RNN), F32)
        h_s[...] = jnp.zeros((ns, D_RNN), F32)
        _mixer_tile(lambda: xm_ref[...], cosm_ref[...], sinm_ref[...], ym_ref, 0, *shared,
                    layer=layer, ns=ns, tt=tt_meta, alpha=alpha)

    @pl.when(step > 0)
    def _():
        if maybe_xt_s:
            xt_s, = maybe_xt_s
            for b in range(ns):
                for c in range(n_slabs):
                    xt_s[c, pl.ds(b, tt, stride=ns), :] = x_ref[b, :, c * LANES:(c + 1) * LANES]
            load_x = lambda: jnp.concatenate([xt_s[c] for c in range(n_slabs)], axis=1)
        else:
            load_x = lambda: x_ref[...]
        _mixer_tile(load_x, cos_ref[...], sin_ref[...], y_ref, tt_meta + (step - 1) * tt, *shared,
                    layer=layer, ns=ns, tt=tt, alpha=alpha)

    @pl.when(step == pl.num_programs(0) - 1)
    def _():
        kwin_out_ref[...] = kwin_s[:, 0:WINDOW, :]
        vwin_out_ref[...] = vwin_s[:, 0:WINDOW, :]
        conv_out_ref[...] = xe_s[0:hist, :]
        h_out_ref[...] = h_s[...]


def _mixer_call(xm, x, tables_m, tables_p, wts, *, layer, ns, tt, alpha):
    batch_major = x.ndim == 3
    rows = x.shape[0] * x.shape[1] if batch_major else x.shape[0]
    r_rows = ns * tt
    n_tiles = rows // r_rows
    rows_m = xm.shape[0]
    tt_meta = rows_m // ns
    hist = (CONV_W - 1) * ns

    def row_spec(c):
        return pl.BlockSpec((r_rows, c), lambda i: (jnp.maximum(i - 1, 0), 0))

    if batch_major:
        x_spec = pl.BlockSpec((ns, tt, D_MODEL), lambda i: (0, jnp.maximum(i - 1, 0), 0))
    else:
        x_spec = row_spec(D_MODEL)
    weights = [wts[k] for k in ('w_in', 'w_attn_proj', 'w_lru_proj', 'w_out', 'conv_w', 'conv_b', 'w_ax',
                                'lru_ba', 'lru_bx', 'lru_lambda', 'ln1_g', 'ln1_b')]
    in_specs = ([pl.BlockSpec(memory_space=pltpu.SMEM), _full_spec(xm.shape), x_spec,
                 _full_spec(tables_m[0].shape), _full_spec(tables_m[1].shape), row_spec(LANES), row_spec(LANES)]
                + [_layer_spec(w, layer) for w in weights])
    out_shape = (jax.ShapeDtypeStruct((rows_m, D_MODEL), F32),
                 jax.ShapeDtypeStruct((rows, D_MODEL), F32),
                 jax.ShapeDtypeStruct((ns, WINDOW, KV_WIDTH), F32),
                 jax.ShapeDtypeStruct((ns, WINDOW, KV_WIDTH), F32),
                 jax.ShapeDtypeStruct((hist, D_RNN), F32),
                 jax.ShapeDtypeStruct((ns, D_RNN), F32))
    out_specs = (pl.BlockSpec((rows_m, D_MODEL), lambda i: (0, 0)),
                 row_spec(D_MODEL),
                 pl.BlockSpec((ns, WINDOW, KV_WIDTH), lambda i: (0, 0, 0)),
                 pl.BlockSpec((ns, WINDOW, KV_WIDTH), lambda i: (0, 0, 0)),
                 pl.BlockSpec((hist, D_RNN), lambda i: (0, 0)),
                 pl.BlockSpec((ns, D_RNN), lambda i: (0, 0)))
    scratch = [pltpu.VMEM((ATTN_WIDTH // LANES, r_rows, LANES), F32),
               pltpu.VMEM((r_rows, KV_WIDTH), F32),
               pltpu.VMEM((r_rows, KV_WIDTH), F32),
               pltpu.VMEM((ATTN_WIDTH // LANES, r_rows, LANES), F32),
               pltpu.VMEM((ns, WINDOW + tt, KV_WIDTH), F32),
               pltpu.VMEM((ns, WINDOW + tt, KV_WIDTH), F32),
               pltpu.VMEM((hist + r_rows, D_RNN), F32),
               pltpu.VMEM((r_rows, D_RNN), F32),
               pltpu.VMEM((r_rows, D_RNN), F32),
               pltpu.VMEM((r_rows, D_RNN), F32),
               pltpu.VMEM((ns, D_RNN), F32)]
    if batch_major:
        scratch.append(pltpu.VMEM((D_MODEL // LANES, r_rows, LANES), F32))
    outs = pl.pallas_call(
        functools.partial(_mixer_kernel, layer=layer, ns=ns, tt=tt, tt_meta=tt_meta, alpha=alpha),
        grid=(n_tiles + 1,),
        in_specs=in_specs,
        out_specs=out_specs,
        out_shape=out_shape,
        scratch_shapes=scratch,
        compiler_params=_COMPILER_PARAMS,
        name=f"mixer_ns{ns}_tt{tt}",
    )(wts['sinks'], xm, x, *tables_m, *tables_p, *weights)
    return outs[0], outs[1], outs[2:]


def _ffn_tile(x, win_ref, wout_ref, g_ref, b_ref, alpha):
    u = _dot(x.astype(BF16), win_ref[...])
    hmid = jax.nn.silu(u[:, :D_FF]) * u[:, D_FF:]
    f = _dot(hmid.astype(BF16), wout_ref[...])
    return _layer_norm(alpha * x + f, g_ref[...], b_ref[...])


def _load_weight_as_bf16(w_hbm, layer, w_s, stage_s, sem):
    rows = w_s.shape[0]
    chunk = stage_s.shape[1]
    n_chunks = rows // chunk

    def copy(i):
        return pltpu.make_async_copy(w_hbm.at[layer, pl.ds(i * chunk, chunk), :],
                                     stage_s.at[i % 2], sem.at[i % 2])

    copy(0).start()
    for i in range(n_chunks):
        if i + 1 < n_chunks:
            copy(i + 1).start()
        copy(i).wait()
        w_s[i * chunk:(i + 1) * chunk, :] = stage_s[i % 2].astype(BF16)


def _ffn_kernel(x_ref, xe_ref, win_hbm, wout_hbm, g_ref, b_ref, y_ref, ye_ref,
                win_ref, wout_ref, stage_in_s, stage_out_s, sem_in, sem_out, *maybe_yt_s, layer, alpha):
    step = pl.program_id(0)
    n_main = pl.num_programs(0) - 1

    @pl.when(step == 0)
    def _():
        _load_weight_as_bf16(win_hbm, layer, win_ref, stage_in_s, sem_in)
        _load_weight_as_bf16(wout_hbm, layer, wout_ref, stage_out_s, sem_out)

    @pl.when(step < n_main)
    def _():
        y = _ffn_tile(x_ref[...], win_ref, wout_ref, g_ref, b_ref, alpha)
        if maybe_yt_s:
            yt_s, = maybe_yt_s
            ns, tt, _ = y_ref.shape
            for c in range(D_MODEL // LANES):
                yt_s[c] = y[:, c * LANES:(c + 1) * LANES]
            for b in range(ns):
                for c in range(D_MODEL // LANES):
                    y_ref[b, :, c * LANES:(c + 1) * LANES] = yt_s[c, pl.ds(b, tt, stride=ns), :]
        else:
            y_ref[...] = y

    @pl.when(step == n_main)
    def _():
        ye_ref[...] = _ffn_tile(xe_ref[...], win_ref, wout_ref, g_ref, b_ref, alpha)


def _ffn_call(x, x_extra, wts, *, layer, tm, alpha, batch_major_out=None):
    rows = x.shape[0]
    n_main = rows // tm
    w_in, w_out = wts['w_ffn_in_f32'], wts['w_ffn_out_f32']
    main_idx = lambda i: jnp.minimum(i, n_main - 1)
    any_spec = pl.BlockSpec(memory_space=pl.ANY)
    scratch = [pltpu.VMEM(w_in.shape[1:], BF16),
               pltpu.VMEM(w_out.shape[1:], BF16),
               pltpu.VMEM((2, w_in.shape[1] // FFN_CAST_CHUNKS, w_in.shape[2]), F32),
               pltpu.VMEM((2, w_out.shape[1] // FFN_CAST_CHUNKS, w_out.shape[2]), F32),
               pltpu.SemaphoreType.DMA((2,)),
               pltpu.SemaphoreType.DMA((2,))]
    if batch_major_out is None:
        out_spec = pl.BlockSpec((tm, D_MODEL), lambda i: (main_idx(i), 0))
        out_shape = jax.ShapeDtypeStruct((rows, D_MODEL), F32)
    else:
        ns = batch_major_out
        out_spec = pl.BlockSpec((ns, tm // ns, D_MODEL), lambda i: (0, main_idx(i), 0))
        out_shape = jax.ShapeDtypeStruct((ns, rows // ns, D_MODEL), F32)
        scratch.append(pltpu.VMEM((D_MODEL // LANES, tm, LANES), F32))
    return pl.pallas_call(
        functools.partial(_ffn_kernel, layer=layer, alpha=alpha),
        grid=(n_main + 1,),
        in_specs=[pl.BlockSpec((tm, D_MODEL), lambda i: (main_idx(i), 0)), _full_spec(x_extra.shape),
                  any_spec, any_spec, _layer_spec(wts['ln2_g'], layer), _layer_spec(wts['ln2_b'], layer)],
        out_specs=(out_spec, pl.BlockSpec(x_extra.shape, lambda i: (0, 0))),
        out_shape=(out_shape, jax.ShapeDtypeStruct(x_extra.shape, F32)),
        scratch_shapes=scratch,
        compiler_params=_COMPILER_PARAMS,
        name=f"ffn_tm{tm}",
    )(x, x_extra, w_in, w_out, wts['ln2_g'], wts['ln2_b'])


def _sample_pre_kernel(x_ref, cos_ref, sin_ref, win_ref,
                       convw_ref, convb_ref, wax_ref, ba_ref, bx_ref, lam_ref,
                       conv0_ref, h0_ref,
                       q_ref, k_ref, v_ref, rec_ref, conv_out_ref, h_out_ref,
                       xe_s, a_s, b_s, hs_s, h_s, *, ns, tt):
    hist = (CONV_W - 1) * ns
    xe_s[0:hist, :] = conv0_ref[...]
    h_s[...] = h0_ref[...]
    xb = x_ref[...].astype(BF16)
    qkv = _dot(xb, win_ref[:, 0:V_END])
    cos = cos_ref[...]
    sin = sin_ref[...]
    q_ref[...] = _rope(qkv[:, :Q_END], cos, sin)
    k_ref[...] = _rope(qkv[:, Q_END:K_END], cos, sin)
    v_ref[...] = qkv[:, K_END:V_END]
    xg = _dot(xb, win_ref[:, V_END:GATE_END])
    rec_ref[...] = _recurrent_branch(xg[:, :D_RNN], xg[:, D_RNN:], xe_s, a_s, b_s, hs_s, h_s,
                                     convw_ref, convb_ref, wax_ref, ba_ref, bx_ref, lam_ref,
                                     ns=ns, tt=tt)
    conv_out_ref[...] = xe_s[0:hist, :]
    h_out_ref[...] = h_s[...]


def _sample_pre_call(x, cos, sin, wts, conv_all, h_all, *, layer, ns, tt):
    rows = ns * tt
    hist = (CONV_W - 1) * ns
    weights = [wts[k] for k in ('w_in', 'conv_w', 'conv_b', 'w_ax', 'lru_ba', 'lru_bx', 'lru_lambda')]
    out_shape = (jax.ShapeDtypeStruct((rows, ATTN_WIDTH), F32),
                 jax.ShapeDtypeStruct((rows, KV_WIDTH), F32),
                 jax.ShapeDtypeStruct((rows, KV_WIDTH), F32),
                 jax.ShapeDtypeStruct((rows, D_RNN), F32),
                 jax.ShapeDtypeStruct((hist, D_RNN), F32),
                 jax.ShapeDtypeStruct((ns, D_RNN), F32))
    return pl.pallas_call(
        functools.partial(_sample_pre_kernel, ns=ns, tt=tt),
        grid=(1,),
        in_specs=([_full_spec(a.shape) for a in (x, cos, sin)]
                  + [_layer_spec(w, layer) for w in weights]
                  + [_layer_spec(conv_all, layer), _layer_spec(h_all, layer)]),
        out_specs=tuple(pl.BlockSpec(s.shape, lambda i, _n=len(s.shape): (0,) * _n) for s in out_shape),
        out_shape=out_shape,
        scratch_shapes=[pltpu.VMEM((hist + rows, D_RNN), F32),
                        pltpu.VMEM((rows, D_RNN), F32),
                        pltpu.VMEM((rows, D_RNN), F32),
                        pltpu.VMEM((rows, D_RNN), F32),
                        pltpu.VMEM((ns, D_RNN), F32)],
        compiler_params=_COMPILER_PARAMS,
        name="sample_pre",
    )(x, cos, sin, *weights, conv_all, h_all)


def _sample_attn_kernel(sink_ref, q_ref, knt_ref, vnt_ref, kc_ref, vc_ref,
                        kprev_ref, vprev_ref, o_ref, ko_ref, vo_ref, *, layer, nb, tt):
    del kprev_ref, vprev_ref
    rows = N_Q_HEADS * tt
    scale = HEAD_DIM ** -0.5
    t_row = lax.broadcasted_iota(jnp.int32, (rows, 2 * WINDOW), 0) & (tt - 1)
    col = lax.broadcasted_iota(jnp.int32, (rows, 2 * WINDOW), 1)
    valid_old = (col < WINDOW) & (col > t_row) & (col >= WINDOW - PAST_LEN)
    lane = lax.broadcasted_iota(jnp.int32, (KV_WIDTH, WINDOW), 1)
    keep_old = lane < WINDOW - tt
    row1 = lax.broadcasted_iota(jnp.int32, (rows, 1), 0)
    sink_col = jnp.full((rows, 1), sink_ref[layer, N_Q_HEADS - 1], F32)
    for h in range(N_Q_HEADS - 2, -1, -1):
        sink_col = jnp.where(row1 < (h + 1) * tt, sink_ref[layer, h], sink_col)
    knt = knt_ref[...]
    vnt = vnt_ref[...]
    knt_b = knt.astype(BF16)
    vnt_b = vnt.astype(BF16)
    for s in range(nb):
        k_old = kc_ref[s]
        v_old = vc_ref[s]
        lo = WINDOW + s * tt
        valid = valid_old | ((col >= lo) & (col <= lo + t_row))
        k_all = jnp.concatenate([k_old.astype(BF16), knt_b], axis=1)
        v_all = jnp.concatenate([v_old.astype(BF16), vnt_b], axis=1)
        sc = jnp.where(valid, _dot(q_ref[s].astype(BF16), k_all) * scale, NEG_INF)
        m = jnp.maximum(jnp.max(sc, axis=-1, keepdims=True), sink_col)
        p = jnp.exp(sc - m)
        denom = jnp.sum(p, axis=-1, keepdims=True) + jnp.exp(sink_col - m)
        o_ref[s] = _dot_tb(p.astype(BF16), v_all) * (1.0 / denom)
        new_shift = (WINDOW - tt - tt * s) % WINDOW
        ko_ref[s] = jnp.where(keep_old, pltpu.roll(k_old, WINDOW - tt, 1), pltpu.roll(knt, new_shift, 1))
        vo_ref[s] = jnp.where(keep_old, pltpu.roll(v_old, WINDOW - tt, 1), pltpu.roll(vnt, new_shift, 1))


def _sample_attn_call(sinks, qz, knt, vnt, kc_all, vc_all, kprev, vprev, *, layer, tt):
    nseq = qz.shape[0]
    nb = WINDOW // tt

    q_spec = pl.BlockSpec((nb,) + tuple(qz.shape[1:]), lambda i: (i, 0, 0))
    cache_spec = pl.BlockSpec((None, nb, KV_WIDTH, WINDOW), lambda i: (layer, i, 0, 0))
    col_spec = pl.BlockSpec((KV_WIDTH, WINDOW), lambda i: (0, i))
    any_spec = pl.BlockSpec(memory_space=pl.ANY)
    out_shape = (jax.ShapeDtypeStruct(qz.shape, F32),
                 jax.ShapeDtypeStruct(kprev.shape, F32),
                 jax.ShapeDtypeStruct(vprev.shape, F32))
    return pl.pallas_call(
        functools.partial(_sample_attn_kernel, layer=layer, nb=nb, tt=tt),
        grid=(nseq // nb,),
        in_specs=[pl.BlockSpec(memory_space=pltpu.SMEM), q_spec, col_spec, col_spec,
                  cache_spec, cache_spec, any_spec, any_spec],
        out_specs=(q_spec, cache_spec, cache_spec),
        out_shape=out_shape,
        input_output_aliases={6: 1, 7: 2},
        compiler_params=_COMPILER_PARAMS,
        name="sample_attn",
    )(sinks, qz, knt, vnt, kc_all, vc_all, kprev, vprev)


def _sample_post_kernel(x_ref, attn_ref, rec_ref, win_ref, wap_ref, wlp_ref, wout_ref,
                        g1_ref, b1_ref, y_ref, *, alpha):
    x = x_ref[...]
    gates = _dot(x.astype(BF16), win_ref[:, GATE_END:D_IN])
    y_ref[...] = _merge_norm(x, gates, attn_ref[...], rec_ref[...],
                             wap_ref, wlp_ref, wout_ref, g1_ref, b1_ref, alpha)


def _sample_post_call(x, attn, rec, wts, *, layer, alpha):
    weights = [wts[k] for k in ('w_in', 'w_attn_proj', 'w_lru_proj', 'w_out', 'ln1_g', 'ln1_b')]
    return pl.pallas_call(
        functools.partial(_sample_post_kernel, alpha=alpha),
        grid=(1,),
        in_specs=[_full_spec(a.shape) for a in (x, attn, rec)] + [_layer_spec(w, layer) for w in weights],
        out_specs=pl.BlockSpec(x.shape, lambda i: (0, 0)),
        out_shape=jax.ShapeDtypeStruct(x.shape, F32),
        compiler_params=_COMPILER_PARAMS,
        name="sample_post",
    )(x, attn, rec, *weights)


def _rope_tables(pos, repeat):
    half = HEAD_DIM // 2
    inv = ROPE_THETA ** (-jnp.arange(half, dtype=F32) / half)
    ang = pos.astype(F32)[:, None] * inv[None, :]
    cos = jnp.cos(ang)
    sin = jnp.sin(ang)
    cos_t = jnp.tile(cos, (1, LANES // half))
    sin_t = jnp.tile(jnp.concatenate([-sin, sin], axis=1), (1, LANES // HEAD_DIM))
    return jnp.repeat(cos_t, repeat, axis=0), jnp.repeat(sin_t, repeat, axis=0)


def _prepare_weights(w_in, w_attn_proj, w_lru_proj, w_out, attn_sinks, conv_w, conv_b,
                     lru_wa, lru_ba, lru_wx, lru_bx, lru_lambda, ln1_g, ln1_b,
                     w_ffn_in, w_ffn_out, ln2_g, ln2_b):
    depth = w_in.shape[0]
    row = lambda v: v.reshape(depth, 1, -1).astype(F32)
    return dict(
        w_in=w_in.astype(BF16),
        w_attn_proj=w_attn_proj.astype(BF16),
        w_lru_proj=w_lru_proj.astype(BF16),
        w_out=w_out.astype(BF16),
        sinks=attn_sinks.astype(F32),
        conv_w=conv_w.astype(F32),
        conv_b=row(conv_b),
        w_ax=jnp.concatenate([lru_wa, lru_wx], axis=-1).astype(BF16),
        lru_ba=row(lru_ba), lru_bx=row(lru_bx), lru_lambda=row(lru_lambda),
        ln1_g=row(ln1_g), ln1_b=row(ln1_b),
        w_ffn_in_f32=w_ffn_in.astype(F32),
        w_ffn_out_f32=w_ffn_out.astype(F32),
        ln2_g=row(ln2_g), ln2_b=row(ln2_b),
    )


def kernel(x_prompt, x_sample, cache_win_k, cache_win_v, state_conv, state_lru, meta_tokens, w_in, w_attn_proj, w_lru_proj, w_out, attn_sinks, conv_w, conv_b, lru_wa, lru_ba, lru_wx, lru_bx, lru_lambda, ln1_g, ln1_b, w_ffn_in, w_ffn_out, ln2_g, ln2_b):
    depth = w_in.shape[0]
    alpha = (2 * depth) ** 0.25
    batch, seq, d = x_prompt.shape
    dec_batch, dec_seq, _ = x_sample.shape
    wts = _prepare_weights(w_in, w_attn_proj, w_lru_proj, w_out, attn_sinks, conv_w, conv_b,
                           lru_wa, lru_ba, lru_wx, lru_bx, lru_lambda, ln1_g, ln1_b,
                           w_ffn_in, w_ffn_out, ln2_g, ln2_b)

    xp = x_prompt
    xm = jnp.broadcast_to(meta_tokens.astype(F32)[:, None, :], (N_META, batch, d)).reshape(N_META * batch, d)
    xs = jnp.transpose(x_sample, (1, 0, 2)).reshape(dec_seq * dec_batch, d)

    cos_m, sin_m = _rope_tables(jnp.arange(N_META, dtype=jnp.int32), batch)
    cos_p, sin_p = _rope_tables(N_META + jnp.arange(seq, dtype=jnp.int32), batch)
    cos_s, sin_s = _rope_tables(PAST_LEN + jnp.arange(dec_seq, dtype=jnp.int32), dec_batch)

    conv_all = jnp.transpose(state_conv, (0, 2, 1, 3)).reshape(depth, (CONV_W - 1) * dec_batch, D_RNN)
    kc_all = jnp.transpose(cache_win_k, (0, 1, 3, 4, 2)).reshape(depth, dec_batch, KV_WIDTH, WINDOW)
    vc_all = jnp.transpose(cache_win_v, (0, 1, 3, 4, 2)).reshape(depth, dec_batch, KV_WIDTH, WINDOW)
    win_k_s = jnp.zeros(kc_all.shape, F32)
    win_v_s = jnp.zeros(vc_all.shape, F32)

    outs = [[] for _ in range(6)]
    for l in range(depth):
        xm1, xp1, (kw, vw, cv, hh) = _mixer_call(xm, xp, (cos_m, sin_m), (cos_p, sin_p), wts,
                                                  layer=l, ns=batch, tt=PROMPT_TT, alpha=alpha)
        outs[0].append(kw.reshape(batch, WINDOW, N_KV_HEADS, HEAD_DIM))
        outs[1].append(vw.reshape(batch, WINDOW, N_KV_HEADS, HEAD_DIM))
        outs[2].append(jnp.transpose(cv.reshape(CONV_W - 1, batch, D_RNN), (1, 0, 2)))
        outs[3].append(hh)

        q, k, v, rec, cv_s, hh_s = _sample_pre_call(xs, cos_s, sin_s, wts, conv_all, state_lru,
                                                    layer=l, ns=dec_batch, tt=dec_seq)
        q5 = jnp.transpose(q.reshape(dec_seq, dec_batch, N_KV_HEADS, Q_PER_KV, HEAD_DIM), (1, 2, 3, 0, 4))
        own_group = jnp.eye(N_KV_HEADS, dtype=bool)[None, :, None, None, :, None]
        qz = jnp.where(own_group, q5[:, :, :, :, None, :], 0.0).reshape(dec_batch, N_Q_HEADS * dec_seq, KV_WIDTH)
        knt = jnp.transpose(k.reshape(dec_seq, dec_batch, KV_WIDTH), (2, 1, 0)).reshape(KV_WIDTH, dec_batch * dec_seq)
        vnt = jnp.transpose(v.reshape(dec_seq, dec_batch, KV_WIDTH), (2, 1, 0)).reshape(KV_WIDTH, dec_batch * dec_seq)
        oz, win_k_s, win_v_s = _sample_attn_call(wts['sinks'], qz, knt, vnt, kc_all, vc_all,
                                                 win_k_s, win_v_s, layer=l, tt=dec_seq)
        o6 = oz.reshape(dec_batch, N_KV_HEADS, Q_PER_KV, dec_seq, N_KV_HEADS, HEAD_DIM)
        o5 = jnp.stack([o6[:, kv, :, :, kv, :] for kv in range(N_KV_HEADS)], axis=1)
        attn = jnp.transpose(o5, (3, 0, 1, 2, 4)).reshape(dec_seq * dec_batch, ATTN_WIDTH)
        xs1 = _sample_post_call(xs, attn, rec, wts, layer=l, alpha=alpha)
        xp, xsm = _ffn_call(xp1, jnp.concatenate([xs1, xm1], axis=0), wts, layer=l, tm=FFN_TM, alpha=alpha,
                            batch_major_out=batch if l == depth - 1 else None)
        xs, xm = xsm[:dec_seq * dec_batch], xsm[dec_seq * dec_batch:]
        outs[4].append(cv_s.reshape(CONV_W - 1, dec_batch, D_RNN))
        outs[5].append(hh_s)

    y_prompt = xp
    y_sample = jnp.transpose(xs.reshape(dec_seq, dec_batch, d), (1, 0, 2))
    st = [jnp.stack(o) for o in outs]
    win_shape = (depth, dec_batch, N_KV_HEADS, HEAD_DIM, WINDOW)
    win_k_sample = jnp.transpose(win_k_s.reshape(win_shape), (0, 1, 4, 2, 3))
    win_v_sample = jnp.transpose(win_v_s.reshape(win_shape), (0, 1, 4, 2, 3))
    conv_sample = jnp.transpose(st[4], (0, 2, 1, 3))
    return (y_prompt, y_sample, st[0], st[1], st[2], st[3], win_k_sample, win_v_sample, conv_sample, st[5])
```

```python
import functools

import jax
import jax.numpy as jnp
from jax import lax
from jax.experimental import pallas as pl
from jax.experimental.pallas import tpu as pltpu

D_MODEL = 1024
N_META = 16
HEAD_DIM = 64
N_Q_HEADS = 8
N_KV_HEADS = 2
Q_PER_KV = N_Q_HEADS // N_KV_HEADS
ATTN_WIDTH = N_Q_HEADS * HEAD_DIM
KV_WIDTH = N_KV_HEADS * HEAD_DIM
WINDOW = 128
ROPE_THETA = 10000.0
D_RNN = D_MODEL
N_LRU_BLOCKS = 8
LRU_BLOCK = D_RNN // N_LRU_BLOCKS
CONV_W = 4
LRU_C = 8.0
D_FF = -(-8 * D_MODEL // (3 * 256)) * 256
LN_EPS = 1e-5
NEG_INF = -1e30
PAST_LEN = 8192

Q_END = ATTN_WIDTH
K_END = Q_END + KV_WIDTH
V_END = K_END + KV_WIDTH
XR_END = V_END + D_RNN
GATE_END = XR_END + D_RNN
D_IN = GATE_END + 2 * D_MODEL

LANES = 128
VMEM_LIMIT_BYTES = 56 * 1024 * 1024

PROMPT_TT = 64
FFN_TM = 512

F32 = jnp.float32
BF16 = jnp.bfloat16
_TRANS_B = (((1,), (1,)), ((), ()))


def _dot(a, b):
    return jnp.dot(a, b, preferred_element_type=F32)


def _dot_tb(a, b):
    return lax.dot_general(a, b, _TRANS_B, preferred_element_type=F32)


def _layer_norm(y, g, b):
    mu = jnp.mean(y, axis=-1, keepdims=True)
    d = y - mu
    var = jnp.mean(d * d, axis=-1, keepdims=True)
    return d * lax.rsqrt(var + LN_EPS) * g + b


def _rope(x, cos, sin_signed):
    lane = lax.broadcasted_iota(jnp.int32, (x.shape[0], LANES), 1)
    first_half = (lane & (HEAD_DIM - 1)) < (HEAD_DIM // 2)
    outs = []
    for g in range(x.shape[1] // LANES):
        xg = x[:, g * LANES:(g + 1) * LANES]
        partner = jnp.where(first_half,
                            pltpu.roll(xg, LANES - HEAD_DIM // 2, 1),
                            pltpu.roll(xg, HEAD_DIM // 2, 1))
        outs.append(xg * cos + partner * sin_signed)
    return outs[0] if len(outs) == 1 else jnp.concatenate(outs, axis=1)


def _softmax_pv(s, valid, sink_col, vv):
    s = jnp.where(valid, s, NEG_INF)
    m = jnp.maximum(jnp.max(s, axis=-1, keepdims=True), sink_col)
    p = jnp.exp(s - m)
    denom = jnp.sum(p, axis=-1, keepdims=True) + jnp.exp(sink_col - m)
    return _dot(p.astype(BF16), vv) * (1.0 / denom)


def _sink_column(sink_ref, layer, kv, tt):
    row = lax.broadcasted_iota(jnp.int32, (Q_PER_KV * tt, 1), 0)
    col = jnp.full((Q_PER_KV * tt, 1), sink_ref[layer, Q_PER_KV * kv + Q_PER_KV - 1], F32)
    for g in range(Q_PER_KV - 2, -1, -1):
        col = jnp.where(row < (g + 1) * tt, sink_ref[layer, Q_PER_KV * kv + g], col)
    return col


def _window_valid(tt, w, jmin):
    t_idx = lax.broadcasted_iota(jnp.int32, (Q_PER_KV * tt, w), 0) & (tt - 1)
    j_idx = lax.broadcasted_iota(jnp.int32, (Q_PER_KV * tt, w), 1)
    return (j_idx > t_idx) & (j_idx <= t_idx + WINDOW) & (j_idx >= jmin)


def _recurrent_branch(xr, gate, xe_ref, a_ref, b_ref, hs_ref, h_ref,
                      convw_ref, convb_ref, wax_ref, ba_ref, bx_ref, lam_ref, *, ns, tt):
    r_rows = tt * ns
    hist = (CONV_W - 1) * ns
    xe_ref[hist:hist + r_rows, :] = xr
    xc = convb_ref[...] + xe_ref[0:r_rows, :] * convw_ref[0:1, :]
    for j in range(1, CONV_W):
        xc = xc + xe_ref[j * ns:j * ns + r_rows, :] * convw_ref[j:j + 1, :]
    new_hist = xe_ref[r_rows:r_rows + hist, :]
    xe_ref[0:hist, :] = new_hist

    xcb = xc.astype(BF16)
    r_parts, i_parts = [], []
    for n in range(N_LRU_BLOCKS):
        ri = _dot(xcb[:, n * LRU_BLOCK:(n + 1) * LRU_BLOCK], wax_ref[n])
        r_parts.append(ri[:, :LRU_BLOCK])
        i_parts.append(ri[:, LRU_BLOCK:])
    r_gate = jax.nn.sigmoid(jnp.concatenate(r_parts, axis=1) + ba_ref[...])
    i_gate = jax.nn.sigmoid(jnp.concatenate(i_parts, axis=1) + bx_ref[...])
    log_a = (-LRU_C) * r_gate * jax.nn.softplus(-lam_ref[...])
    a = jnp.exp(log_a)
    a_ref[...] = a
    y = -jnp.tanh(log_a) * (a * a + 1.0)
    b_ref[...] = jnp.where(y > 0.0, y * lax.rsqrt(y), 0.0) * (i_gate * xc)

    h = h_ref[...]
    for t in range(tt):
        h = a_ref[t * ns:(t + 1) * ns, :] * h + b_ref[t * ns:(t + 1) * ns, :]
        hs_ref[t * ns:(t + 1) * ns, :] = h
    h_ref[...] = h
    return hs_ref[...] * jax.nn.gelu(gate)


def _merge_norm(x, gates, attn, rec, wap_ref, wlp_ref, wout_ref, g1_ref, b1_ref, alpha):
    ap = _dot(attn.astype(BF16), wap_ref[...])
    lp = _dot(rec.astype(BF16), wlp_ref[...])
    merged = (jax.nn.sigmoid(gates[:, :D_MODEL]) * ap
              + jax.nn.sigmoid(gates[:, D_MODEL:]) * lp)
    mix = _dot(merged.astype(BF16), wout_ref[...])
    return _layer_norm(alpha * x + mix, g1_ref[...], b1_ref[...])


def _layer_spec(arr, layer):
    nd = arr.ndim
    return pl.BlockSpec((None,) + tuple(arr.shape[1:]), lambda *_: (layer,) + (0,) * (nd - 1),
                        pipeline_mode=pl.Buffered(1))


def _full_spec(shape):
    nd = len(shape)
    return pl.BlockSpec(tuple(shape), lambda *_: (0,) * nd, pipeline_mode=pl.Buffered(1))


_COMPILER_PARAMS = pltpu.CompilerParams(dimension_semantics=("arbitrary",),
                                        vmem_limit_bytes=VMEM_LIMIT_BYTES)


def _mixer_tile(load_x, cos, sin, y_ref, pos_start, sink_ref, win_ref, wap_ref, wlp_ref, wout_ref,
                convw_ref, convb_ref, wax_ref, ba_ref, bx_ref, lam_ref, g1_ref, b1_ref,
                q_s, k_s, v_s, attn_s, kwin_s, vwin_s, xe_s, a_s, b_s, hs_s, h_s, *, layer, ns, tt, alpha):
    r_rows = ns * tt
    n_qg = ATTN_WIDTH // LANES
    proj = _dot(load_x().astype(BF16), win_ref[...])
    qkv = proj[:, 0:V_END]
    cos_q = cos * (HEAD_DIM ** -0.5)
    sin_q = sin * (HEAD_DIM ** -0.5)
    for j in range(n_qg):
        q_s[j, 0:r_rows, :] = _rope(qkv[:, j * LANES:(j + 1) * LANES], cos_q, sin_q)
    k_s[0:r_rows, :] = _rope(qkv[:, Q_END:K_END], cos, sin)
    v_s[0:r_rows, :] = qkv[:, K_END:V_END]

    valid = _window_valid(tt, WINDOW + tt, WINDOW - pos_start)
    sink_cols = [_sink_column(sink_ref, layer, kv, tt) for kv in range(N_KV_HEADS)]
    for b in range(ns):
        rows = pl.ds(b, tt, stride=ns)
        kwin_s[b, WINDOW:WINDOW + tt, :] = k_s[rows, :]
        vwin_s[b, WINDOW:WINDOW + tt, :] = v_s[rows, :]
        keys = kwin_s[b, 0:WINDOW + tt, :]
        vals = vwin_s[b, 0:WINDOW + tt, :]
        kb = keys.astype(BF16)
        vb = vals.astype(BF16)
        q_b = [q_s[j, rows, :].astype(BF16) for j in range(n_qg)]
        heads = [q_b[h // 2][:, (h % 2) * HEAD_DIM:(h % 2 + 1) * HEAD_DIM] for h in range(N_Q_HEADS)]
        outs = []
        for kv in range(N_KV_HEADS):
            kk = kb[:, kv * HEAD_DIM:(kv + 1) * HEAD_DIM]
            vv = vb[:, kv * HEAD_DIM:(kv + 1) * HEAD_DIM]
            qs = jnp.concatenate(heads[Q_PER_KV * kv:Q_PER_KV * (kv + 1)], axis=0)
            o = _softmax_pv(_dot_tb(qs, kk), valid, sink_cols[kv], vv)
            outs.extend(o[g * tt:(g + 1) * tt] for g in range(Q_PER_KV))
        for j in range(n_qg):
            attn_s[j, rows, :] = jnp.concatenate(outs[2 * j:2 * j + 2], axis=1)
        kwin_s[b, 0:WINDOW, :] = keys[tt:tt + WINDOW]
        vwin_s[b, 0:WINDOW, :] = vals[tt:tt + WINDOW]

    tile_rows = pl.ds(0, r_rows)
    rec = _recurrent_branch(proj[:, V_END:XR_END], proj[:, XR_END:GATE_END], xe_s,
                            a_s.at[tile_rows], b_s.at[tile_rows], hs_s.at[tile_rows], h_s,
                            convw_ref, convb_ref, wax_ref, ba_ref, bx_ref, lam_ref, ns=ns, tt=tt)
    attn = jnp.concatenate([attn_s[j, 0:r_rows, :] for j in range(n_qg)], axis=1)
    y_ref[...] = _merge_norm(load_x(), proj[:, GATE_END:D_IN], attn, rec, wap_ref, wlp_ref, wout_ref,
                             g1_ref, b1_ref, alpha)


def _mixer_kernel(sink_ref, xm_ref, x_ref, cosm_ref, sinm_ref, cos_ref, sin_ref,
                  win_ref, wap_ref, wlp_ref, wout_ref,
                  convw_ref, convb_ref, wax_ref, ba_ref, bx_ref, lam_ref, g1_ref, b1_ref,
                  ym_ref, y_ref, kwin_out_ref, vwin_out_ref, conv_out_ref, h_out_ref,
                  q_s, k_s, v_s, attn_s, kwin_s, vwin_s, xe_s, a_s, b_s, hs_s, h_s, *maybe_xt_s,
                  layer, ns, tt, tt_meta, alpha):
    step = pl.program_id(0)
    hist = (CONV_W - 1) * ns
    n_slabs = D_MODEL // LANES
    shared = (sink_ref, win_ref, wap_ref, wlp_ref, wout_ref,
              convw_ref, convb_ref, wax_ref, ba_ref, bx_ref, lam_ref, g1_ref, b1_ref,
              q_s, k_s, v_s, attn_s, kwin_s, vwin_s, xe_s, a_s, b_s, hs_s, h_s)

    @pl.when(step == 0)
    def _():
        kwin_s[:, 0:WINDOW, :] = jnp.zeros((ns, WINDOW, KV_WIDTH), F32)
        vwin_s[:, 0:WINDOW, :] = jnp.zeros((ns, WINDOW, KV_WIDTH), F32)
        xe_s[0:hist, :] = jnp.zeros((hist, D_RNN), F32)
        h_s[...] = jnp.zeros((ns, D_RNN), F32)
        _mixer_tile(lambda: xm_ref[...], cosm_ref[...], sinm_ref[...], ym_ref, 0, *shared,
                    layer=layer, ns=ns, tt=tt_meta, alpha=alpha)

    @pl.when(step > 0)
    def _():
        if maybe_xt_s:
            xt_s, = maybe_xt_s
            for b in range(ns):
                for c in range(n_slabs):
                    xt_s[c, pl.ds(b, tt, stride=ns), :] = x_ref[b, :, c * LANES:(c + 1) * LANES]
            load_x = lambda: jnp.concatenate([xt_s[c] for c in range(n_slabs)], axis=1)
        else:
            load_x = lambda: x_ref[...]
        _mixer_tile(load_x, cos_ref[...], sin_ref[...], y_ref, tt_meta + (step - 1) * tt, *shared,
                    layer=layer, ns=ns, tt=tt, alpha=alpha)

    @pl.when(step == pl.num_programs(0) - 1)
    def _():
        kwin_out_ref[...] = kwin_s[:, 0:WINDOW, :]
        vwin_out_ref[...] = vwin_s[:, 0:WINDOW, :]
        conv_out_ref[...] = xe_s[0:hist, :]
        h_out_ref[...] = h_s[...]


def _mixer_call(xm, x, tables_m, tables_p, wts, *, layer, ns, tt, alpha):
    batch_major = x.ndim == 3
    rows = x.shape[0] * x.shape[1] if batch_major else x.shape[0]
    r_rows = ns * tt
    n_tiles = rows // r_rows
    rows_m = xm.shape[0]
    tt_meta = rows_m // ns
    hist = (CONV_W - 1) * ns

    def row_spec(c):
        return pl.BlockSpec((r_rows, c), lambda i: (jnp.maximum(i - 1, 0), 0))

    if batch_major:
        x_spec = pl.BlockSpec((ns, tt, D_MODEL), lambda i: (0, jnp.maximum(i - 1, 0), 0))
    else:
        x_spec = row_spec(D_MODEL)
    weights = [wts[k] for k in ('w_in', 'w_attn_proj', 'w_lru_proj', 'w_out', 'conv_w', 'conv_b', 'w_ax',
                                'lru_ba', 'lru_bx', 'lru_lambda', 'ln1_g', 'ln1_b')]
    in_specs = ([pl.BlockSpec(memory_space=pltpu.SMEM), _full_spec(xm.shape), x_spec,
                 _full_spec(tables_m[0].shape), _full_spec(tables_m[1].shape), row_spec(LANES), row_spec(LANES)]
                + [_layer_spec(w, layer) for w in weights])
    out_shape = (jax.ShapeDtypeStruct((rows_m, D_MODEL), F32),
                 jax.ShapeDtypeStruct((rows, D_MODEL), F32),
                 jax.ShapeDtypeStruct((ns, WINDOW, KV_WIDTH), F32),
                 jax.ShapeDtypeStruct((ns, WINDOW, KV_WIDTH), F32),
                 jax.ShapeDtypeStruct((hist, D_RNN), F32),
                 jax.ShapeDtypeStruct((ns, D_RNN), F32))
    out_specs = (pl.BlockSpec((rows_m, D_MODEL), lambda i: (0, 0)),
                 row_spec(D_MODEL),
                 pl.BlockSpec((ns, WINDOW, KV_WIDTH), lambda i: (0, 0, 0)),
                 pl.BlockSpec((ns, WINDOW, KV_WIDTH), lambda i: (0, 0, 0)),
                 pl.BlockSpec((hist, D_RNN), lambda i: (0, 0)),
                 pl.BlockSpec((ns, D_RNN), lambda i: (0, 0)))
    scratch = [pltpu.VMEM((ATTN_WIDTH // LANES, r_rows, LANES), F32),
               pltpu.VMEM((r_rows, KV_WIDTH), F32),
               pltpu.VMEM((r_rows, KV_WIDTH), F32),
               pltpu.VMEM((ATTN_WIDTH // LANES, r_rows, LANES), F32),
               pltpu.VMEM((ns, WINDOW + tt, KV_WIDTH), F32),
               pltpu.VMEM((ns, WINDOW + tt, KV_WIDTH), F32),
               pltpu.VMEM((hist + r_rows, D_RNN), F32),
               pltpu.VMEM((r_rows, D_RNN), F32),
               pltpu.VMEM((r_rows, D_RNN), F32),
               pltpu.VMEM((r_rows, D_RNN), F32),
               pltpu.VMEM((ns, D_RNN), F32)]
    if batch_major:
        scratch.append(pltpu.VMEM((D_MODEL // LANES, r_rows, LANES), F32))
    outs = pl.pallas_call(
        functools.partial(_mixer_kernel, layer=layer, ns=ns, tt=tt, tt_meta=tt_meta, alpha=alpha),
        grid=(n_tiles + 1,),
        in_specs=in_specs,
        out_specs=out_specs,
        out_shape=out_shape,
        scratch_shapes=scratch,
        compiler_params=_COMPILER_PARAMS,
        name=f"mixer_ns{ns}_tt{tt}",
    )(wts['sinks'], xm, x, *tables_m, *tables_p, *weights)
    return outs[0], outs[1], outs[2:]


def _ffn_tile(x, win_ref, wout_ref, g_ref, b_ref, alpha):
    u = _dot(x.astype(BF16), win_ref[...])
    hmid = jax.nn.silu(u[:, :D_FF]) * u[:, D_FF:]
    f = _dot(hmid.astype(BF16), wout_ref[...])
    return _layer_norm(alpha * x + f, g_ref[...], b_ref[...])


def _ffn_kernel(x_ref, xa_ref, xb_ref, win_ref, wout_ref, g_ref, b_ref, y_ref, ya_ref, yb_ref,
                *maybe_yt_s, alpha):
    step = pl.program_id(0)
    n_main = pl.num_programs(0) - 2

    @pl.when(step < n_main)
    def _():
        y = _ffn_tile(x_ref[...], win_ref, wout_ref, g_ref, b_ref, alpha)
        if maybe_yt_s:
            yt_s, = maybe_yt_s
            ns, tt, _ = y_ref.shape
            for c in range(D_MODEL // LANES):
                yt_s[c] = y[:, c * LANES:(c + 1) * LANES]
            for b in range(ns):
                for c in range(D_MODEL // LANES):
                    y_ref[b, :, c * LANES:(c + 1) * LANES] = yt_s[c, pl.ds(b, tt, stride=ns), :]
        else:
            y_ref[...] = y

    @pl.when(step == n_main)
    def _():
        ya_ref[...] = _ffn_tile(xa_ref[...], win_ref, wout_ref, g_ref, b_ref, alpha)

    @pl.when(step == n_main + 1)
    def _():
        yb_ref[...] = _ffn_tile(xb_ref[...], win_ref, wout_ref, g_ref, b_ref, alpha)


def _ffn_call(x, x_a, x_b, wts, *, layer, tm, alpha, batch_major_out=None):
    rows = x.shape[0]
    n_main = rows // tm
    weights = [wts[k] for k in ('w_ffn_in', 'w_ffn_out', 'ln2_g', 'ln2_b')]
    main_idx = lambda i: jnp.minimum(i, n_main - 1)
    if batch_major_out is None:
        out_spec = pl.BlockSpec((tm, D_MODEL), lambda i: (main_idx(i), 0))
        out_shape = jax.ShapeDtypeStruct((rows, D_MODEL), F32)
        scratch = []
    else:
        ns = batch_major_out
        out_spec = pl.BlockSpec((ns, tm // ns, D_MODEL), lambda i: (0, main_idx(i), 0))
        out_shape = jax.ShapeDtypeStruct((ns, rows // ns, D_MODEL), F32)
        scratch = [pltpu.VMEM((D_MODEL // LANES, tm, LANES), F32)]
    return pl.pallas_call(
        functools.partial(_ffn_kernel, alpha=alpha),
        grid=(n_main + 2,),
        in_specs=([pl.BlockSpec((tm, D_MODEL), lambda i: (main_idx(i), 0)),
                   _full_spec(x_a.shape), _full_spec(x_b.shape)]
                  + [_layer_spec(w, layer) for w in weights]),
        out_specs=(out_spec, pl.BlockSpec(x_a.shape, lambda i: (0, 0)), pl.BlockSpec(x_b.shape, lambda i: (0, 0))),
        out_shape=(out_shape, jax.ShapeDtypeStruct(x_a.shape, F32), jax.ShapeDtypeStruct(x_b.shape, F32)),
        scratch_shapes=scratch,
        compiler_params=_COMPILER_PARAMS,
        name=f"ffn_tm{tm}",
    )(x, x_a, x_b, *weights)


def _sample_pre_kernel(x_ref, cos_ref, sin_ref, win_ref,
                       convw_ref, convb_ref, wax_ref, ba_ref, bx_ref, lam_ref,
                       conv0_ref, h0_ref,
                       q_ref, k_ref, v_ref, rec_ref, conv_out_ref, h_out_ref,
                       xe_s, a_s, b_s, hs_s, h_s, *, ns, tt):
    hist = (CONV_W - 1) * ns
    xe_s[0:hist, :] = conv0_ref[...]
    h_s[...] = h0_ref[...]
    xb = x_ref[...].astype(BF16)
    qkv = _dot(xb, win_ref[:, 0:V_END])
    cos = cos_ref[...]
    sin = sin_ref[...]
    q_ref[...] = _rope(qkv[:, :Q_END], cos, sin)
    k_ref[...] = _rope(qkv[:, Q_END:K_END], cos, sin)
    v_ref[...] = qkv[:, K_END:V_END]
    xg = _dot(xb, win_ref[:, V_END:GATE_END])
    rec_ref[...] = _recurrent_branch(xg[:, :D_RNN], xg[:, D_RNN:], xe_s, a_s, b_s, hs_s, h_s,
                                     convw_ref, convb_ref, wax_ref, ba_ref, bx_ref, lam_ref,
                                     ns=ns, tt=tt)
    conv_out_ref[...] = xe_s[0:hist, :]
    h_out_ref[...] = h_s[...]


def _sample_pre_call(x, cos, sin, wts, conv_all, h_all, *, layer, ns, tt):
    rows = ns * tt
    hist = (CONV_W - 1) * ns
    weights = [wts[k] for k in ('w_in', 'conv_w', 'conv_b', 'w_ax', 'lru_ba', 'lru_bx', 'lru_lambda')]
    out_shape = (jax.ShapeDtypeStruct((rows, ATTN_WIDTH), F32),
                 jax.ShapeDtypeStruct((rows, KV_WIDTH), F32),
                 jax.ShapeDtypeStruct((rows, KV_WIDTH), F32),
                 jax.ShapeDtypeStruct((rows, D_RNN), F32),
                 jax.ShapeDtypeStruct((hist, D_RNN), F32),
                 jax.ShapeDtypeStruct((ns, D_RNN), F32))
    return pl.pallas_call(
        functools.partial(_sample_pre_kernel, ns=ns, tt=tt),
        grid=(1,),
        in_specs=([_full_spec(a.shape) for a in (x, cos, sin)]
                  + [_layer_spec(w, layer) for w in weights]
                  + [_layer_spec(conv_all, layer), _layer_spec(h_all, layer)]),
        out_specs=tuple(pl.BlockSpec(s.shape, lambda i, _n=len(s.shape): (0,) * _n) for s in out_shape),
        out_shape=out_shape,
        scratch_shapes=[pltpu.VMEM((hist + rows, D_RNN), F32),
                        pltpu.VMEM((rows, D_RNN), F32),
                        pltpu.VMEM((rows, D_RNN), F32),
                        pltpu.VMEM((rows, D_RNN), F32),
                        pltpu.VMEM((ns, D_RNN), F32)],
        compiler_params=_COMPILER_PARAMS,
        name="sample_pre",
    )(x, cos, sin, *weights, conv_all, h_all)


def _sample_attn_kernel(sink_ref, q_ref, knt_ref, vnt_ref, kc_ref, vc_ref,
                        kprev_ref, vprev_ref, o_ref, ko_ref, vo_ref, *, layer, nb, tt):
    del kprev_ref, vprev_ref
    rows = N_Q_HEADS * tt
    scale = HEAD_DIM ** -0.5
    t_row = lax.broadcasted_iota(jnp.int32, (rows, 2 * WINDOW), 0) & (tt - 1)
    col = lax.broadcasted_iota(jnp.int32, (rows, 2 * WINDOW), 1)
    valid_old = (col < WINDOW) & (col > t_row) & (col >= WINDOW - PAST_LEN)
    lane = lax.broadcasted_iota(jnp.int32, (KV_WIDTH, WINDOW), 1)
    keep_old = lane < WINDOW - tt
    row1 = lax.broadcasted_iota(jnp.int32, (rows, 1), 0)
    sink_col = jnp.full((rows, 1), sink_ref[layer, N_Q_HEADS - 1], F32)
    for h in range(N_Q_HEADS - 2, -1, -1):
        sink_col = jnp.where(row1 < (h + 1) * tt, sink_ref[layer, h], sink_col)
    knt = knt_ref[...]
    vnt = vnt_ref[...]
    knt_b = knt.astype(BF16)
    vnt_b = vnt.astype(BF16)
    for s in range(nb):
        k_old = kc_ref[s]
        v_old = vc_ref[s]
        lo = WINDOW + s * tt
        valid = valid_old | ((col >= lo) & (col <= lo + t_row))
        k_all = jnp.concatenate([k_old.astype(BF16), knt_b], axis=1)
        v_all = jnp.concatenate([v_old.astype(BF16), vnt_b], axis=1)
        sc = jnp.where(valid, _dot(q_ref[s].astype(BF16), k_all) * scale, NEG_INF)
        m = jnp.maximum(jnp.max(sc, axis=-1, keepdims=True), sink_col)
        p = jnp.exp(sc - m)
        denom = jnp.sum(p, axis=-1, keepdims=True) + jnp.exp(sink_col - m)
        o_ref[s] = _dot_tb(p.astype(BF16), v_all) * (1.0 / denom)
        new_shift = (WINDOW - tt - tt * s) % WINDOW
        ko_ref[s] = jnp.where(keep_old, pltpu.roll(k_old, WINDOW - tt, 1), pltpu.roll(knt, new_shift, 1))
        vo_ref[s] = jnp.where(keep_old, pltpu.roll(v_old, WINDOW - tt, 1), pltpu.roll(vnt, new_shift, 1))


def _sample_attn_call(sinks, qz, knt, vnt, kc_all, vc_all, kprev, vprev, *, layer, tt):
    nseq = qz.shape[0]
    nb = WINDOW // tt

    q_spec = pl.BlockSpec((nb,) + tuple(qz.shape[1:]), lambda i: (i, 0, 0))
    cache_spec = pl.BlockSpec((None, nb, KV_WIDTH, WINDOW), lambda i: (layer, i, 0, 0))
    col_spec = pl.BlockSpec((KV_WIDTH, WINDOW), lambda i: (0, i))
    any_spec = pl.BlockSpec(memory_space=pl.ANY)
    out_shape = (jax.ShapeDtypeStruct(qz.shape, F32),
                 jax.ShapeDtypeStruct(kprev.shape, F32),
                 jax.ShapeDtypeStruct(vprev.shape, F32))
    return pl.pallas_call(
        functools.partial(_sample_attn_kernel, layer=layer, nb=nb, tt=tt),
        grid=(nseq // nb,),
        in_specs=[pl.BlockSpec(memory_space=pltpu.SMEM), q_spec, col_spec, col_spec,
                  cache_spec, cache_spec, any_spec, any_spec],
        out_specs=(q_spec, cache_spec, cache_spec),
        out_shape=out_shape,
        input_output_aliases={6: 1, 7: 2},
        compiler_params=_COMPILER_PARAMS,
        name="sample_attn",
    )(sinks, qz, knt, vnt, kc_all, vc_all, kprev, vprev)


def _sample_post_kernel(x_ref, attn_ref, rec_ref, win_ref, wap_ref, wlp_ref, wout_ref,
                        g1_ref, b1_ref, y_ref, *, alpha):
    x = x_ref[...]
    gates = _dot(x.astype(BF16), win_ref[:, GATE_END:D_IN])
    y_ref[...] = _merge_norm(x, gates, attn_ref[...], rec_ref[...],
                             wap_ref, wlp_ref, wout_ref, g1_ref, b1_ref, alpha)


def _sample_post_call(x, attn, rec, wts, *, layer, alpha):
    weights = [wts[k] for k in ('w_in', 'w_attn_proj', 'w_lru_proj', 'w_out', 'ln1_g', 'ln1_b')]
    return pl.pallas_call(
        functools.partial(_sample_post_kernel, alpha=alpha),
        grid=(1,),
        in_specs=[_full_spec(a.shape) for a in (x, attn, rec)] + [_layer_spec(w, layer) for w in weights],
        out_specs=pl.BlockSpec(x.shape, lambda i: (0, 0)),
        out_shape=jax.ShapeDtypeStruct(x.shape, F32),
        compiler_params=_COMPILER_PARAMS,
        name="sample_post",
    )(x, attn, rec, *weights)


def _rope_tables(pos, repeat):
    half = HEAD_DIM // 2
    inv = ROPE_THETA ** (-jnp.arange(half, dtype=F32) / half)
    ang = pos.astype(F32)[:, None] * inv[None, :]
    cos = jnp.cos(ang)
    sin = jnp.sin(ang)
    cos_t = jnp.tile(cos, (1, LANES // half))
    sin_t = jnp.tile(jnp.concatenate([-sin, sin], axis=1), (1, LANES // HEAD_DIM))
    return jnp.repeat(cos_t, repeat, axis=0), jnp.repeat(sin_t, repeat, axis=0)


def _prepare_weights(w_in, w_attn_proj, w_lru_proj, w_out, attn_sinks, conv_w, conv_b,
                     lru_wa, lru_ba, lru_wx, lru_bx, lru_lambda, ln1_g, ln1_b,
                     w_ffn_in, w_ffn_out, ln2_g, ln2_b):
    depth = w_in.shape[0]
    row = lambda v: v.reshape(depth, 1, -1).astype(F32)
    return dict(
        w_in=w_in.astype(BF16),
        w_attn_proj=w_attn_proj.astype(BF16),
        w_lru_proj=w_lru_proj.astype(BF16),
        w_out=w_out.astype(BF16),
        sinks=attn_sinks.astype(F32),
        conv_w=conv_w.astype(F32),
        conv_b=row(conv_b),
        w_ax=jnp.concatenate([lru_wa, lru_wx], axis=-1).astype(BF16),
        lru_ba=row(lru_ba), lru_bx=row(lru_bx), lru_lambda=row(lru_lambda),
        ln1_g=row(ln1_g), ln1_b=row(ln1_b),
        w_ffn_in=w_ffn_in.astype(BF16),
        w_ffn_out=w_ffn_out.astype(BF16),
        ln2_g=row(ln2_g), ln2_b=row(ln2_b),
    )


def kernel(x_prompt, x_sample, cache_win_k, cache_win_v, state_conv, state_lru, meta_tokens, w_in, w_attn_proj, w_lru_proj, w_out, attn_sinks, conv_w, conv_b, lru_wa, lru_ba, lru_wx, lru_bx, lru_lambda, ln1_g, ln1_b, w_ffn_in, w_ffn_out, ln2_g, ln2_b):
    depth = w_in.shape[0]
    alpha = (2 * depth) ** 0.25
    batch, seq, d = x_prompt.shape
    dec_batch, dec_seq, _ = x_sample.shape
    wts = _prepare_weights(w_in, w_attn_proj, w_lru_proj, w_out, attn_sinks, conv_w, conv_b,
                           lru_wa, lru_ba, lru_wx, lru_bx, lru_lambda, ln1_g, ln1_b,
                           w_ffn_in, w_ffn_out, ln2_g, ln2_b)

    xp = x_prompt
    xm = jnp.broadcast_to(meta_tokens.astype(F32)[:, None, :], (N_META, batch, d)).reshape(N_META * batch, d)
    xs = jnp.transpose(x_sample, (1, 0, 2)).reshape(dec_seq * dec_batch, d)

    cos_m, sin_m = _rope_tables(jnp.arange(N_META, dtype=jnp.int32), batch)
    cos_p, sin_p = _rope_tables(N_META + jnp.arange(seq, dtype=jnp.int32), batch)
    cos_s, sin_s = _rope_tables(PAST_LEN + jnp.arange(dec_seq, dtype=jnp.int32), dec_batch)

    conv_all = jnp.transpose(state_conv, (0, 2, 1, 3)).reshape(depth, (CONV_W - 1) * dec_batch, D_RNN)
    kc_all = jnp.transpose(cache_win_k, (0, 1, 3, 4, 2)).reshape(depth, dec_batch, KV_WIDTH, WINDOW)
    vc_all = jnp.transpose(cache_win_v, (0, 1, 3, 4, 2)).reshape(depth, dec_batch, KV_WIDTH, WINDOW)
    win_k_s = jnp.zeros(kc_all.shape, F32)
    win_v_s = jnp.zeros(vc_all.shape, F32)

    outs = [[] for _ in range(6)]
    for l in range(depth):
        xm1, xp1, (kw, vw, cv, hh) = _mixer_call(xm, xp, (cos_m, sin_m), (cos_p, sin_p), wts,
                                                  layer=l, ns=batch, tt=PROMPT_TT, alpha=alpha)
        outs[0].append(kw.reshape(batch, WINDOW, N_KV_HEADS, HEAD_DIM))
        outs[1].append(vw.reshape(batch, WINDOW, N_KV_HEADS, HEAD_DIM))
        outs[2].append(jnp.transpose(cv.reshape(CONV_W - 1, batch, D_RNN), (1, 0, 2)))
        outs[3].append(hh)

        q, k, v, rec, cv_s, hh_s = _sample_pre_call(xs, cos_s, sin_s, wts, conv_all, state_lru,
                                                    layer=l, ns=dec_batch, tt=dec_seq)
        q5 = jnp.transpose(q.reshape(dec_seq, dec_batch, N_KV_HEADS, Q_PER_KV, HEAD_DIM), (1, 2, 3, 0, 4))
        own_group = jnp.eye(N_KV_HEADS, dtype=bool)[None, :, None, None, :, None]
        qz = jnp.where(own_group, q5[:, :, :, :, None, :], 0.0).reshape(dec_batch, N_Q_HEADS * dec_seq, KV_WIDTH)
        knt = jnp.transpose(k.reshape(dec_seq, dec_batch, KV_WIDTH), (2, 1, 0)).reshape(KV_WIDTH, dec_batch * dec_seq)
        vnt = jnp.transpose(v.reshape(dec_seq, dec_batch, KV_WIDTH), (2, 1, 0)).reshape(KV_WIDTH, dec_batch * dec_seq)
        oz, win_k_s, win_v_s = _sample_attn_call(wts['sinks'], qz, knt, vnt, kc_all, vc_all,
                                                 win_k_s, win_v_s, layer=l, tt=dec_seq)
        o6 = oz.reshape(dec_batch, N_KV_HEADS, Q_PER_KV, dec_seq, N_KV_HEADS, HEAD_DIM)
        o5 = jnp.stack([o6[:, kv, :, :, kv, :] for kv in range(N_KV_HEADS)], axis=1)
        attn = jnp.transpose(o5, (3, 0, 1, 2, 4)).reshape(dec_seq * dec_batch, ATTN_WIDTH)
        xs1 = _sample_post_call(xs, attn, rec, wts, layer=l, alpha=alpha)
        xp, xs, xm = _ffn_call(xp1, xs1, xm1, wts, layer=l, tm=FFN_TM, alpha=alpha,
                               batch_major_out=batch if l == depth - 1 else None)
        outs[4].append(cv_s.reshape(CONV_W - 1, dec_batch, D_RNN))
        outs[5].append(hh_s)

    y_prompt = xp
    y_sample = jnp.transpose(xs.reshape(dec_seq, dec_batch, d), (1, 0, 2))
    st = [jnp.stack(o) for o in outs]
    win_shape = (depth, dec_batch, N_KV_HEADS, HEAD_DIM, WINDOW)
    win_k_sample = jnp.transpose(win_k_s.reshape(win_shape), (0, 1, 4, 2, 3))
    win_v_sample = jnp.transpose(win_v_s.reshape(win_shape), (0, 1, 4, 2, 3))
    conv_sample = jnp.transpose(st[4], (0, 2, 1, 3))
    return (y_prompt, y_sample, st[0], st[1], st[2], st[3], win_k_sample, win_v_sample, conv_sample, st[5])
```

```python
import functools

import jax
import jax.numpy as jnp
from jax import lax
from jax.experimental import pallas as pl
from jax.experimental.pallas import tpu as pltpu

D_MODEL = 1024
N_META = 16
HEAD_DIM = 64
N_Q_HEADS = 8
N_KV_HEADS = 2
Q_PER_KV = N_Q_HEADS // N_KV_HEADS
ATTN_WIDTH = N_Q_HEADS * HEAD_DIM
KV_WIDTH = N_KV_HEADS * HEAD_DIM
WINDOW = 128
ROPE_THETA = 10000.0
D_RNN = D_MODEL
N_LRU_BLOCKS = 8
LRU_BLOCK = D_RNN // N_LRU_BLOCKS
CONV_W = 4
LRU_C = 8.0
D_FF = -(-8 * D_MODEL // (3 * 256)) * 256
LN_EPS = 1e-5
NEG_INF = -1e30
PAST_LEN = 8192

Q_END = ATTN_WIDTH
K_END = Q_END + KV_WIDTH
V_END = K_END + KV_WIDTH
XR_END = V_END + D_RNN
GATE_END = XR_END + D_RNN
D_IN = GATE_END + 2 * D_MODEL

LANES = 128
VMEM_LIMIT_BYTES = 56 * 1024 * 1024

PROMPT_TT = 64
FFN_TM = 256

F32 = jnp.float32
BF16 = jnp.bfloat16
_TRANS_B = (((1,), (1,)), ((), ()))


def _dot(a, b):
    return jnp.dot(a, b, preferred_element_type=F32)


def _dot_tb(a, b):
    return lax.dot_general(a, b, _TRANS_B, preferred_element_type=F32)


def _layer_norm(y, g, b):
    mu = jnp.mean(y, axis=-1, keepdims=True)
    d = y - mu
    var = jnp.mean(d * d, axis=-1, keepdims=True)
    return d * lax.rsqrt(var + LN_EPS) * g + b


def _rope(x, cos, sin_signed):
    lane = lax.broadcasted_iota(jnp.int32, (x.shape[0], LANES), 1)
    first_half = (lane & (HEAD_DIM - 1)) < (HEAD_DIM // 2)
    outs = []
    for g in range(x.shape[1] // LANES):
        xg = x[:, g * LANES:(g + 1) * LANES]
        partner = jnp.where(first_half,
                            pltpu.roll(xg, LANES - HEAD_DIM // 2, 1),
                            pltpu.roll(xg, HEAD_DIM // 2, 1))
        outs.append(xg * cos + partner * sin_signed)
    return outs[0] if len(outs) == 1 else jnp.concatenate(outs, axis=1)


def _softmax_pv(s, valid, sink_col, vv):
    s = jnp.where(valid, s, NEG_INF)
    m = jnp.maximum(jnp.max(s, axis=-1, keepdims=True), sink_col)
    p = jnp.exp(s - m)
    denom = jnp.sum(p, axis=-1, keepdims=True) + jnp.exp(sink_col - m)
    return _dot(p.astype(BF16), vv) * (1.0 / denom)


def _sink_column(sink_ref, layer, kv, tt):
    row = lax.broadcasted_iota(jnp.int32, (Q_PER_KV * tt, 1), 0)
    col = jnp.full((Q_PER_KV * tt, 1), sink_ref[layer, Q_PER_KV * kv + Q_PER_KV - 1], F32)
    for g in range(Q_PER_KV - 2, -1, -1):
        col = jnp.where(row < (g + 1) * tt, sink_ref[layer, Q_PER_KV * kv + g], col)
    return col


def _window_valid(tt, w, jmin):
    t_idx = lax.broadcasted_iota(jnp.int32, (Q_PER_KV * tt, w), 0) & (tt - 1)
    j_idx = lax.broadcasted_iota(jnp.int32, (Q_PER_KV * tt, w), 1)
    return (j_idx > t_idx) & (j_idx <= t_idx + WINDOW) & (j_idx >= jmin)


def _recurrent_branch(xr, gate, xe_ref, a_ref, b_ref, hs_ref, h_ref,
                      convw_ref, convb_ref, wax_ref, ba_ref, bx_ref, lam_ref, *, ns, tt):
    r_rows = tt * ns
    hist = (CONV_W - 1) * ns
    xe_ref[hist:hist + r_rows, :] = xr
    xc = convb_ref[...] + xe_ref[0:r_rows, :] * convw_ref[0:1, :]
    for j in range(1, CONV_W):
        xc = xc + xe_ref[j * ns:j * ns + r_rows, :] * convw_ref[j:j + 1, :]
    new_hist = xe_ref[r_rows:r_rows + hist, :]
    xe_ref[0:hist, :] = new_hist

    xcb = xc.astype(BF16)
    r_parts, i_parts = [], []
    for n in range(N_LRU_BLOCKS):
        ri = _dot(xcb[:, n * LRU_BLOCK:(n + 1) * LRU_BLOCK], wax_ref[n])
        r_parts.append(ri[:, :LRU_BLOCK])
        i_parts.append(ri[:, LRU_BLOCK:])
    r_gate = jax.nn.sigmoid(jnp.concatenate(r_parts, axis=1) + ba_ref[...])
    i_gate = jax.nn.sigmoid(jnp.concatenate(i_parts, axis=1) + bx_ref[...])
    log_a = (-LRU_C) * r_gate * jax.nn.softplus(-lam_ref[...])
    a = jnp.exp(log_a)
    a_ref[...] = a
    y = -jnp.tanh(log_a) * (a * a + 1.0)
    b_ref[...] = jnp.where(y > 0.0, y * lax.rsqrt(y), 0.0) * (i_gate * xc)

    h = h_ref[...]
    for t in range(tt):
        h = a_ref[t * ns:(t + 1) * ns, :] * h + b_ref[t * ns:(t + 1) * ns, :]
        hs_ref[t * ns:(t + 1) * ns, :] = h
    h_ref[...] = h
    return hs_ref[...] * jax.nn.gelu(gate)


def _merge_norm(x, gates, attn, rec, wap_ref, wlp_ref, wout_ref, g1_ref, b1_ref, alpha):
    ap = _dot(attn.astype(BF16), wap_ref[...])
    lp = _dot(rec.astype(BF16), wlp_ref[...])
    merged = (jax.nn.sigmoid(gates[:, :D_MODEL]) * ap
              + jax.nn.sigmoid(gates[:, D_MODEL:]) * lp)
    mix = _dot(merged.astype(BF16), wout_ref[...])
    return _layer_norm(alpha * x + mix, g1_ref[...], b1_ref[...])


def _layer_spec(arr, layer):
    nd = arr.ndim
    return pl.BlockSpec((None,) + tuple(arr.shape[1:]), lambda *_: (layer,) + (0,) * (nd - 1),
                        pipeline_mode=pl.Buffered(1))


def _full_spec(shape):
    nd = len(shape)
    return pl.BlockSpec(tuple(shape), lambda *_: (0,) * nd, pipeline_mode=pl.Buffered(1))


_COMPILER_PARAMS = pltpu.CompilerParams(dimension_semantics=("arbitrary",),
                                        vmem_limit_bytes=VMEM_LIMIT_BYTES)


def _mixer_tile(load_x, cos, sin, y_ref, pos_start, sink_ref, win_ref, wap_ref, wlp_ref, wout_ref,
                convw_ref, convb_ref, wax_ref, ba_ref, bx_ref, lam_ref, g1_ref, b1_ref,
                q_s, k_s, v_s, attn_s, kwin_s, vwin_s, xe_s, a_s, b_s, hs_s, h_s, *, layer, ns, tt, alpha):
    r_rows = ns * tt
    n_qg = ATTN_WIDTH // LANES
    proj = _dot(load_x().astype(BF16), win_ref[...])
    qkv = proj[:, 0:V_END]
    cos_q = cos * (HEAD_DIM ** -0.5)
    sin_q = sin * (HEAD_DIM ** -0.5)
    for j in range(n_qg):
        q_s[j, 0:r_rows, :] = _rope(qkv[:, j * LANES:(j + 1) * LANES], cos_q, sin_q)
    k_s[0:r_rows, :] = _rope(qkv[:, Q_END:K_END], cos, sin)
    v_s[0:r_rows, :] = qkv[:, K_END:V_END]

    valid = _window_valid(tt, WINDOW + tt, WINDOW - pos_start)
    sink_cols = [_sink_column(sink_ref, layer, kv, tt) for kv in range(N_KV_HEADS)]
    for b in range(ns):
        rows = pl.ds(b, tt, stride=ns)
        kwin_s[b, WINDOW:WINDOW + tt, :] = k_s[rows, :]
        vwin_s[b, WINDOW:WINDOW + tt, :] = v_s[rows, :]
        keys = kwin_s[b, 0:WINDOW + tt, :]
        vals = vwin_s[b, 0:WINDOW + tt, :]
        kb = keys.astype(BF16)
        vb = vals.astype(BF16)
        q_b = [q_s[j, rows, :].astype(BF16) for j in range(n_qg)]
        heads = [q_b[h // 2][:, (h % 2) * HEAD_DIM:(h % 2 + 1) * HEAD_DIM] for h in range(N_Q_HEADS)]
        outs = []
        for kv in range(N_KV_HEADS):
            kk = kb[:, kv * HEAD_DIM:(kv + 1) * HEAD_DIM]
            vv = vb[:, kv * HEAD_DIM:(kv + 1) * HEAD_DIM]
            qs = jnp.concatenate(heads[Q_PER_KV * kv:Q_PER_KV * (kv + 1)], axis=0)
            o = _softmax_pv(_dot_tb(qs, kk), valid, sink_cols[kv], vv)
            outs.extend(o[g * tt:(g + 1) * tt] for g in range(Q_PER_KV))
        for j in range(n_qg):
            attn_s[j, rows, :] = jnp.concatenate(outs[2 * j:2 * j + 2], axis=1)
        kwin_s[b, 0:WINDOW, :] = keys[tt:tt + WINDOW]
        vwin_s[b, 0:WINDOW, :] = vals[tt:tt + WINDOW]

    tile_rows = pl.ds(0, r_rows)
    rec = _recurrent_branch(proj[:, V_END:XR_END], proj[:, XR_END:GATE_END], xe_s,
                            a_s.at[tile_rows], b_s.at[tile_rows], hs_s.at[tile_rows], h_s,
                            convw_ref, convb_ref, wax_ref, ba_ref, bx_ref, lam_ref, ns=ns, tt=tt)
    attn = jnp.concatenate([attn_s[j, 0:r_rows, :] for j in range(n_qg)], axis=1)
    y_ref[...] = _merge_norm(load_x(), proj[:, GATE_END:D_IN], attn, rec, wap_ref, wlp_ref, wout_ref,
                             g1_ref, b1_ref, alpha)


def _mixer_kernel(sink_ref, xm_ref, x_ref, cosm_ref, sinm_ref, cos_ref, sin_ref,
                  win_ref, wap_ref, wlp_ref, wout_ref,
                  convw_ref, convb_ref, wax_ref, ba_ref, bx_ref, lam_ref, g1_ref, b1_ref,
                  ym_ref, y_ref, kwin_out_ref, vwin_out_ref, conv_out_ref, h_out_ref,
                  q_s, k_s, v_s, attn_s, kwin_s, vwin_s, xe_s, a_s, b_s, hs_s, h_s, *maybe_xt_s,
                  layer, ns, tt, tt_meta, alpha):
    step = pl.program_id(0)
    hist = (CONV_W - 1) * ns
    n_slabs = D_MODEL // LANES
    shared = (sink_ref, win_ref, wap_ref, wlp_ref, wout_ref,
              convw_ref, convb_ref, wax_ref, ba_ref, bx_ref, lam_ref, g1_ref, b1_ref,
              q_s, k_s, v_s, attn_s, kwin_s, vwin_s, xe_s, a_s, b_s, hs_s, h_s)

    @pl.when(step == 0)
    def _():
        kwin_s[:, 0:WINDOW, :] = jnp.zeros((ns, WINDOW, KV_WIDTH), F32)
        vwin_s[:, 0:WINDOW, :] = jnp.zeros((ns, WINDOW, KV_WIDTH), F32)
        xe_s[0:hist, :] = jnp.zeros((hist, D_RNN), F32)
        h_s[...] = jnp.zeros((ns, D_RNN), F32)
        _mixer_tile(lambda: xm_ref[...], cosm_ref[...], sinm_ref[...], ym_ref, 0, *shared,
                    layer=layer, ns=ns, tt=tt_meta, alpha=alpha)

    @pl.when(step > 0)
    def _():
        if maybe_xt_s:
            xt_s, = maybe_xt_s
            for b in range(ns):
                for c in range(n_slabs):
                    xt_s[c, pl.ds(b, tt, stride=ns), :] = x_ref[b, :, c * LANES:(c + 1) * LANES]
            load_x = lambda: jnp.concatenate([xt_s[c] for c in range(n_slabs)], axis=1)
        else:
            load_x = lambda: x_ref[...]
        _mixer_tile(load_x, cos_ref[...], sin_ref[...], y_ref, tt_meta + (step - 1) * tt, *shared,
                    layer=layer, ns=ns, tt=tt, alpha=alpha)

    @pl.when(step == pl.num_programs(0) - 1)
    def _():
        kwin_out_ref[...] = kwin_s[:, 0:WINDOW, :]
        vwin_out_ref[...] = vwin_s[:, 0:WINDOW, :]
        conv_out_ref[...] = xe_s[0:hist, :]
        h_out_ref[...] = h_s[...]


def _mixer_call(xm, x, tables_m, tables_p, wts, *, layer, ns, tt, alpha):
    batch_major = x.ndim == 3
    rows = x.shape[0] * x.shape[1] if batch_major else x.shape[0]
    r_rows = ns * tt
    n_tiles = rows // r_rows
    rows_m = xm.shape[0]
    tt_meta = rows_m // ns
    hist = (CONV_W - 1) * ns

    def row_spec(c):
        return pl.BlockSpec((r_rows, c), lambda i: (jnp.maximum(i - 1, 0), 0))

    if batch_major:
        x_spec = pl.BlockSpec((ns, tt, D_MODEL), lambda i: (0, jnp.maximum(i - 1, 0), 0))
    else:
        x_spec = row_spec(D_MODEL)
    weights = [wts[k] for k in ('w_in', 'w_attn_proj', 'w_lru_proj', 'w_out', 'conv_w', 'conv_b', 'w_ax',
                                'lru_ba', 'lru_bx', 'lru_lambda', 'ln1_g', 'ln1_b')]
    in_specs = ([pl.BlockSpec(memory_space=pltpu.SMEM), _full_spec(xm.shape), x_spec,
                 _full_spec(tables_m[0].shape), _full_spec(tables_m[1].shape), row_spec(LANES), row_spec(LANES)]
                + [_layer_spec(w, layer) for w in weights])
    out_shape = (jax.ShapeDtypeStruct((rows_m, D_MODEL), F32),
                 jax.ShapeDtypeStruct((rows, D_MODEL), F32),
                 jax.ShapeDtypeStruct((ns, WINDOW, KV_WIDTH), F32),
                 jax.ShapeDtypeStruct((ns, WINDOW, KV_WIDTH), F32),
                 jax.ShapeDtypeStruct((hist, D_RNN), F32),
                 jax.ShapeDtypeStruct((ns, D_RNN), F32))
    out_specs = (pl.BlockSpec((rows_m, D_MODEL), lambda i: (0, 0)),
                 row_spec(D_MODEL),
                 pl.BlockSpec((ns, WINDOW, KV_WIDTH), lambda i: (0, 0, 0)),
                 pl.BlockSpec((ns, WINDOW, KV_WIDTH), lambda i: (0, 0, 0)),
                 pl.BlockSpec((hist, D_RNN), lambda i: (0, 0)),
                 pl.BlockSpec((ns, D_RNN), lambda i: (0, 0)))
    scratch = [pltpu.VMEM((ATTN_WIDTH // LANES, r_rows, LANES), F32),
               pltpu.VMEM((r_rows, KV_WIDTH), F32),
               pltpu.VMEM((r_rows, KV_WIDTH), F32),
               pltpu.VMEM((ATTN_WIDTH // LANES, r_rows, LANES), F32),
               pltpu.VMEM((ns, WINDOW + tt, KV_WIDTH), F32),
               pltpu.VMEM((ns, WINDOW + tt, KV_WIDTH), F32),
               pltpu.VMEM((hist + r_rows, D_RNN), F32),
               pltpu.VMEM((r_rows, D_RNN), F32),
               pltpu.VMEM((r_rows, D_RNN), F32),
               pltpu.VMEM((r_rows, D_RNN), F32),
               pltpu.VMEM((ns, D_RNN), F32)]
    if batch_major:
        scratch.append(pltpu.VMEM((D_MODEL // LANES, r_rows, LANES), F32))
    outs = pl.pallas_call(
        functools.partial(_mixer_kernel, layer=layer, ns=ns, tt=tt, tt_meta=tt_meta, alpha=alpha),
        grid=(n_tiles + 1,),
        in_specs=in_specs,
        out_specs=out_specs,
        out_shape=out_shape,
        scratch_shapes=scratch,
        compiler_params=_COMPILER_PARAMS,
        name=f"mixer_ns{ns}_tt{tt}",
    )(wts['sinks'], xm, x, *tables_m, *tables_p, *weights)
    return outs[0], outs[1], outs[2:]


def _ffn_tile(x, win_ref, wout_ref, g_ref, b_ref, alpha):
    u = _dot(x.astype(BF16), win_ref[...])
    hmid = jax.nn.silu(u[:, :D_FF]) * u[:, D_FF:]
    f = _dot(hmid.astype(BF16), wout_ref[...])
    return _layer_norm(alpha * x + f, g_ref[...], b_ref[...])


def _ffn_kernel(x_ref, xa_ref, xb_ref, win_ref, wout_ref, g_ref, b_ref, y_ref, ya_ref, yb_ref,
                *maybe_yt_s, alpha):
    step = pl.program_id(0)
    n_main = pl.num_programs(0) - 2

    @pl.when(step < n_main)
    def _():
        y = _ffn_tile(x_ref[...], win_ref, wout_ref, g_ref, b_ref, alpha)
        if maybe_yt_s:
            yt_s, = maybe_yt_s
            ns, tt, _ = y_ref.shape
            for c in range(D_MODEL // LANES):
                yt_s[c] = y[:, c * LANES:(c + 1) * LANES]
            for b in range(ns):
                for c in range(D_MODEL // LANES):
                    y_ref[b, :, c * LANES:(c + 1) * LANES] = yt_s[c, pl.ds(b, tt, stride=ns), :]
        else:
            y_ref[...] = y

    @pl.when(step == n_main)
    def _():
        ya_ref[...] = _ffn_tile(xa_ref[...], win_ref, wout_ref, g_ref, b_ref, alpha)

    @pl.when(step == n_main + 1)
    def _():
        yb_ref[...] = _ffn_tile(xb_ref[...], win_ref, wout_ref, g_ref, b_ref, alpha)


def _ffn_call(x, x_a, x_b, wts, *, layer, tm, alpha, batch_major_out=None):
    rows = x.shape[0]
    n_main = rows // tm
    weights = [wts[k] for k in ('w_ffn_in', 'w_ffn_out', 'ln2_g', 'ln2_b')]
    main_idx = lambda i: jnp.minimum(i, n_main - 1)
    if batch_major_out is None:
        out_spec = pl.BlockSpec((tm, D_MODEL), lambda i: (main_idx(i), 0))
        out_shape = jax.ShapeDtypeStruct((rows, D_MODEL), F32)
        scratch = []
    else:
        ns = batch_major_out
        out_spec = pl.BlockSpec((ns, tm // ns, D_MODEL), lambda i: (0, main_idx(i), 0))
        out_shape = jax.ShapeDtypeStruct((ns, rows // ns, D_MODEL), F32)
        scratch = [pltpu.VMEM((D_MODEL // LANES, tm, LANES), F32)]
    return pl.pallas_call(
        functools.partial(_ffn_kernel, alpha=alpha),
        grid=(n_main + 2,),
        in_specs=([pl.BlockSpec((tm, D_MODEL), lambda i: (main_idx(i), 0)),
                   _full_spec(x_a.shape), _full_spec(x_b.shape)]
                  + [_layer_spec(w, layer) for w in weights]),
        out_specs=(out_spec, pl.BlockSpec(x_a.shape, lambda i: (0, 0)), pl.BlockSpec(x_b.shape, lambda i: (0, 0))),
        out_shape=(out_shape, jax.ShapeDtypeStruct(x_a.shape, F32), jax.ShapeDtypeStruct(x_b.shape, F32)),
        scratch_shapes=scratch,
        compiler_params=_COMPILER_PARAMS,
        name=f"ffn_tm{tm}",
    )(x, x_a, x_b, *weights)


def _sample_pre_kernel(x_ref, cos_ref, sin_ref, win_ref,
                       convw_ref, convb_ref, wax_ref, ba_ref, bx_ref, lam_ref,
                       conv0_ref, h0_ref,
                       q_ref, k_ref, v_ref, rec_ref, conv_out_ref, h_out_ref,
                       xe_s, a_s, b_s, hs_s, h_s, *, ns, tt):
    hist = (CONV_W - 1) * ns
    xe_s[0:hist, :] = conv0_ref[...]
    h_s[...] = h0_ref[...]
    xb = x_ref[...].astype(BF16)
    qkv = _dot(xb, win_ref[:, 0:V_END])
    cos = cos_ref[...]
    sin = sin_ref[...]
    q_ref[...] = _rope(qkv[:, :Q_END], cos, sin)
    k_ref[...] = _rope(qkv[:, Q_END:K_END], cos, sin)
    v_ref[...] = qkv[:, K_END:V_END]
    xg = _dot(xb, win_ref[:, V_END:GATE_END])
    rec_ref[...] = _recurrent_branch(xg[:, :D_RNN], xg[:, D_RNN:], xe_s, a_s, b_s, hs_s, h_s,
                                     convw_ref, convb_ref, wax_ref, ba_ref, bx_ref, lam_ref,
                                     ns=ns, tt=tt)
    conv_out_ref[...] = xe_s[0:hist, :]
    h_out_ref[...] = h_s[...]


def _sample_pre_call(x, cos, sin, wts, conv_all, h_all, *, layer, ns, tt):
    rows = ns * tt
    hist = (CONV_W - 1) * ns
    weights = [wts[k] for k in ('w_in', 'conv_w', 'conv_b', 'w_ax', 'lru_ba', 'lru_bx', 'lru_lambda')]
    out_shape = (jax.ShapeDtypeStruct((rows, ATTN_WIDTH), F32),
                 jax.ShapeDtypeStruct((rows, KV_WIDTH), F32),
                 jax.ShapeDtypeStruct((rows, KV_WIDTH), F32),
                 jax.ShapeDtypeStruct((rows, D_RNN), F32),
                 jax.ShapeDtypeStruct((hist, D_RNN), F32),
                 jax.ShapeDtypeStruct((ns, D_RNN), F32))
    return pl.pallas_call(
        functools.partial(_sample_pre_kernel, ns=ns, tt=tt),
        grid=(1,),
        in_specs=([_full_spec(a.shape) for a in (x, cos, sin)]
                  + [_layer_spec(w, layer) for w in weights]
                  + [_layer_spec(conv_all, layer), _layer_spec(h_all, layer)]),
        out_specs=tuple(pl.BlockSpec(s.shape, lambda i, _n=len(s.shape): (0,) * _n) for s in out_shape),
        out_shape=out_shape,
        scratch_shapes=[pltpu.VMEM((hist + rows, D_RNN), F32),
                        pltpu.VMEM((rows, D_RNN), F32),
                        pltpu.VMEM((rows, D_RNN), F32),
                        pltpu.VMEM((rows, D_RNN), F32),
                        pltpu.VMEM((ns, D_RNN), F32)],
        compiler_params=_COMPILER_PARAMS,
        name="sample_pre",
    )(x, cos, sin, *weights, conv_all, h_all)


def _sample_attn_kernel(sink_ref, q_ref, knt_ref, vnt_ref, kc_ref, vc_ref,
                        kprev_ref, vprev_ref, o_ref, ko_ref, vo_ref, *, layer, nb, tt):
    del kprev_ref, vprev_ref
    rows = N_Q_HEADS * tt
    scale = HEAD_DIM ** -0.5
    t_row = lax.broadcasted_iota(jnp.int32, (rows, 2 * WINDOW), 0) & (tt - 1)
    col = lax.broadcasted_iota(jnp.int32, (rows, 2 * WINDOW), 1)
    valid_old = (col < WINDOW) & (col > t_row) & (col >= WINDOW - PAST_LEN)
    lane = lax.broadcasted_iota(jnp.int32, (KV_WIDTH, WINDOW), 1)
    keep_old = lane < WINDOW - tt
    row1 = lax.broadcasted_iota(jnp.int32, (rows, 1), 0)
    sink_col = jnp.full((rows, 1), sink_ref[layer, N_Q_HEADS - 1], F32)
    for h in range(N_Q_HEADS - 2, -1, -1):
        sink_col = jnp.where(row1 < (h + 1) * tt, sink_ref[layer, h], sink_col)
    knt = knt_ref[...]
    vnt = vnt_ref[...]
    knt_b = knt.astype(BF16)
    vnt_b = vnt.astype(BF16)
    for s in range(nb):
        k_old = kc_ref[s]
        v_old = vc_ref[s]
        lo = WINDOW + s * tt
        valid = valid_old | ((col >= lo) & (col <= lo + t_row))
        k_all = jnp.concatenate([k_old.astype(BF16), knt_b], axis=1)
        v_all = jnp.concatenate([v_old.astype(BF16), vnt_b], axis=1)
        sc = jnp.where(valid, _dot(q_ref[s].astype(BF16), k_all) * scale, NEG_INF)
        m = jnp.maximum(jnp.max(sc, axis=-1, keepdims=True), sink_col)
        p = jnp.exp(sc - m)
        denom = jnp.sum(p, axis=-1, keepdims=True) + jnp.exp(sink_col - m)
        o_ref[s] = _dot_tb(p.astype(BF16), v_all) * (1.0 / denom)
        new_shift = (WINDOW - tt - tt * s) % WINDOW
        ko_ref[s] = jnp.where(keep_old, pltpu.roll(k_old, WINDOW - tt, 1), pltpu.roll(knt, new_shift, 1))
        vo_ref[s] = jnp.where(keep_old, pltpu.roll(v_old, WINDOW - tt, 1), pltpu.roll(vnt, new_shift, 1))


def _sample_attn_call(sinks, qz, knt, vnt, kc_all, vc_all, kprev, vprev, *, layer, tt):
    nseq = qz.shape[0]
    nb = WINDOW // tt

    q_spec = pl.BlockSpec((nb,) + tuple(qz.shape[1:]), lambda i: (i, 0, 0))
    cache_spec = pl.BlockSpec((None, nb, KV_WIDTH, WINDOW), lambda i: (layer, i, 0, 0))
    col_spec = pl.BlockSpec((KV_WIDTH, WINDOW), lambda i: (0, i))
    any_spec = pl.BlockSpec(memory_space=pl.ANY)
    out_shape = (jax.ShapeDtypeStruct(qz.shape, F32),
                 jax.ShapeDtypeStruct(kprev.shape, F32),
                 jax.ShapeDtypeStruct(vprev.shape, F32))
    return pl.pallas_call(
        functools.partial(_sample_attn_kernel, layer=layer, nb=nb, tt=tt),
        grid=(nseq // nb,),
        in_specs=[pl.BlockSpec(memory_space=pltpu.SMEM), q_spec, col_spec, col_spec,
                  cache_spec, cache_spec, any_spec, any_spec],
        out_specs=(q_spec, cache_spec, cache_spec),
        out_shape=out_shape,
        input_output_aliases={6: 1, 7: 2},
        compiler_params=_COMPILER_PARAMS,
        name="sample_attn",
    )(sinks, qz, knt, vnt, kc_all, vc_all, kprev, vprev)


def _sample_post_kernel(x_ref, attn_ref, rec_ref, win_ref, wap_ref, wlp_ref, wout_ref,
                        g1_ref, b1_ref, y_ref, *, alpha):
    x = x_ref[...]
    gates = _dot(x.astype(BF16), win_ref[:, GATE_END:D_IN])
    y_ref[...] = _merge_norm(x, gates, attn_ref[...], rec_ref[...],
                             wap_ref, wlp_ref, wout_ref, g1_ref, b1_ref, alpha)


def _sample_post_call(x, attn, rec, wts, *, layer, alpha):
    weights = [wts[k] for k in ('w_in', 'w_attn_proj', 'w_lru_proj', 'w_out', 'ln1_g', 'ln1_b')]
    return pl.pallas_call(
        functools.partial(_sample_post_kernel, alpha=alpha),
        grid=(1,),
        in_specs=[_full_spec(a.shape) for a in (x, attn, rec)] + [_layer_spec(w, layer) for w in weights],
        out_specs=pl.BlockSpec(x.shape, lambda i: (0, 0)),
        out_shape=jax.ShapeDtypeStruct(x.shape, F32),
        compiler_params=_COMPILER_PARAMS,
        name="sample_post",
    )(x, attn, rec, *weights)


def _rope_tables(pos, repeat):
    half = HEAD_DIM // 2
    inv = ROPE_THETA ** (-jnp.arange(half, dtype=F32) / half)
    ang = pos.astype(F32)[:, None] * inv[None, :]
    cos = jnp.cos(ang)
    sin = jnp.sin(ang)
    cos_t = jnp.tile(cos, (1, LANES // half))
    sin_t = jnp.tile(jnp.concatenate([-sin, sin], axis=1), (1, LANES // HEAD_DIM))
    return jnp.repeat(cos_t, repeat, axis=0), jnp.repeat(sin_t, repeat, axis=0)


def _prepare_weights(w_in, w_attn_proj, w_lru_proj, w_out, attn_sinks, conv_w, conv_b,
                     lru_wa, lru_ba, lru_wx, lru_bx, lru_lambda, ln1_g, ln1_b,
                     w_ffn_in, w_ffn_out, ln2_g, ln2_b):
    depth = w_in.shape[0]
    row = lambda v: v.reshape(depth, 1, -1).astype(F32)
    return dict(
        w_in=w_in.astype(BF16),
        w_attn_proj=w_attn_proj.astype(BF16),
        w_lru_proj=w_lru_proj.astype(BF16),
        w_out=w_out.astype(BF16),
        sinks=attn_sinks.astype(F32),
        conv_w=conv_w.astype(F32),
        conv_b=row(conv_b),
        w_ax=jnp.concatenate([lru_wa, lru_wx], axis=-1).astype(BF16),
        lru_ba=row(lru_ba), lru_bx=row(lru_bx), lru_lambda=row(lru_lambda),
        ln1_g=row(ln1_g), ln1_b=row(ln1_b),
        w_ffn_in=w_ffn_in.astype(BF16),
        w_ffn_out=w_ffn_out.astype(BF16),
        ln2_g=row(ln2_g), ln2_b=row(ln2_b),
    )


def kernel(x_prompt, x_sample, cache_win_k, cache_win_v, state_conv, state_lru, meta_tokens, w_in, w_attn_proj, w_lru_proj, w_out, attn_sinks, conv_w, conv_b, lru_wa, lru_ba, lru_wx, lru_bx, lru_lambda, ln1_g, ln1_b, w_ffn_in, w_ffn_out, ln2_g, ln2_b):
    depth = w_in.shape[0]
    alpha = (2 * depth) ** 0.25
    batch, seq, d = x_prompt.shape
    dec_batch, dec_seq, _ = x_sample.shape
    wts = _prepare_weights(w_in, w_attn_proj, w_lru_proj, w_out, attn_sinks, conv_w, conv_b,
                           lru_wa, lru_ba, lru_wx, lru_bx, lru_lambda, ln1_g, ln1_b,
                           w_ffn_in, w_ffn_out, ln2_g, ln2_b)

    xp = x_prompt
    xm = jnp.broadcast_to(meta_tokens.astype(F32)[:, None, :], (N_META, batch, d)).reshape(N_META * batch, d)
    xs = jnp.transpose(x_sample, (1, 0, 2)).reshape(dec_seq * dec_batch, d)

    cos_m, sin_m = _rope_tables(jnp.arange(N_META, dtype=jnp.int32), batch)
    cos_p, sin_p = _rope_tables(N_META + jnp.arange(seq, dtype=jnp.int32), batch)
    cos_s, sin_s = _rope_tables(PAST_LEN + jnp.arange(dec_seq, dtype=jnp.int32), dec_batch)

    conv_all = jnp.transpose(state_conv, (0, 2, 1, 3)).reshape(depth, (CONV_W - 1) * dec_batch, D_RNN)
    kc_all = jnp.transpose(cache_win_k, (0, 1, 3, 4, 2)).reshape(depth, dec_batch, KV_WIDTH, WINDOW)
    vc_all = jnp.transpose(cache_win_v, (0, 1, 3, 4, 2)).reshape(depth, dec_batch, KV_WIDTH, WINDOW)
    win_k_s = jnp.zeros(kc_all.shape, F32)
    win_v_s = jnp.zeros(vc_all.shape, F32)

    outs = [[] for _ in range(6)]
    for l in range(depth):
        xm1, xp1, (kw, vw, cv, hh) = _mixer_call(xm, xp, (cos_m, sin_m), (cos_p, sin_p), wts,
                                                  layer=l, ns=batch, tt=PROMPT_TT, alpha=alpha)
        outs[0].append(kw.reshape(batch, WINDOW, N_KV_HEADS, HEAD_DIM))
        outs[1].append(vw.reshape(batch, WINDOW, N_KV_HEADS, HEAD_DIM))
        outs[2].append(jnp.transpose(cv.reshape(CONV_W - 1, batch, D_RNN), (1, 0, 2)))
        outs[3].append(hh)

        q, k, v, rec, cv_s, hh_s = _sample_pre_call(xs, cos_s, sin_s, wts, conv_all, state_lru,
                                                    layer=l, ns=dec_batch, tt=dec_seq)
        q5 = jnp.transpose(q.reshape(dec_seq, dec_batch, N_KV_HEADS, Q_PER_KV, HEAD_DIM), (1, 2, 3, 0, 4))
        own_group = jnp.eye(N_KV_HEADS, dtype=bool)[None, :, None, None, :, None]
        qz = jnp.where(own_group, q5[:, :, :, :, None, :], 0.0).reshape(dec_batch, N_Q_HEADS * dec_seq, KV_WIDTH)
        knt = jnp.transpose(k.reshape(dec_seq, dec_batch, KV_WIDTH), (2, 1, 0)).reshape(KV_WIDTH, dec_batch * dec_seq)
        vnt = jnp.transpose(v.reshape(dec_seq, dec_batch, KV_WIDTH), (2, 1, 0)).reshape(KV_WIDTH, dec_batch * dec_seq)
        oz, win_k_s, win_v_s = _sample_attn_call(wts['sinks'], qz, knt, vnt, kc_all, vc_all,
                                                 win_k_s, win_v_s, layer=l, tt=dec_seq)
        o6 = oz.reshape(dec_batch, N_KV_HEADS, Q_PER_KV, dec_seq, N_KV_HEADS, HEAD_DIM)
        o5 = jnp.stack([o6[:, kv, :, :, kv, :] for kv in range(N_KV_HEADS)], axis=1)
        attn = jnp.transpose(o5, (3, 0, 1, 2, 4)).reshape(dec_seq * dec_batch, ATTN_WIDTH)
        xs1 = _sample_post_call(xs, attn, rec, wts, layer=l, alpha=alpha)
        xp, xs, xm = _ffn_call(xp1, xs1, xm1, wts, layer=l, tm=FFN_TM, alpha=alpha,
                               batch_major_out=batch if l == depth - 1 else None)
        outs[4].append(cv_s.reshape(CONV_W - 1, dec_batch, D_RNN))
        outs[5].append(hh_s)

    y_prompt = xp
    y_sample = jnp.transpose(xs.reshape(dec_seq, dec_batch, d), (1, 0, 2))
    st = [jnp.stack(o) for o in outs]
    win_shape = (depth, dec_batch, N_KV_HEADS, HEAD_DIM, WINDOW)
    win_k_sample = jnp.transpose(win_k_s.reshape(win_shape), (0, 1, 4, 2, 3))
    win_v_sample = jnp.transpose(win_v_s.reshape(win_shape), (0, 1, 4, 2, 3))
    conv_sample = jnp.transpose(st[4], (0, 2, 1, 3))
    return (y_prompt, y_sample, st[0], st[1], st[2], st[3], win_k_sample, win_v_sample, conv_sample, st[5])
```

```python
import functools

import jax
import jax.numpy as jnp
from jax import lax
from jax.experimental import pallas as pl
from jax.experimental.pallas import tpu as pltpu

D_MODEL = 1024
N_META = 16
HEAD_DIM = 64
N_Q_HEADS = 8
N_KV_HEADS = 2
Q_PER_KV = N_Q_HEADS // N_KV_HEADS
ATTN_WIDTH = N_Q_HEADS * HEAD_DIM
KV_WIDTH = N_KV_HEADS * HEAD_DIM
WINDOW = 128
ROPE_THETA = 10000.0
D_RNN = D_MODEL
N_LRU_BLOCKS = 8
LRU_BLOCK = D_RNN // N_LRU_BLOCKS
CONV_W = 4
LRU_C = 8.0
D_FF = -(-8 * D_MODEL // (3 * 256)) * 256
LN_EPS = 1e-5
NEG_INF = -1e30
PAST_LEN = 8192

Q_END = ATTN_WIDTH
K_END = Q_END + KV_WIDTH
V_END = K_END + KV_WIDTH
XR_END = V_END + D_RNN
GATE_END = XR_END + D_RNN
D_IN = GATE_END + 2 * D_MODEL

LANES = 128
VMEM_LIMIT_BYTES = 56 * 1024 * 1024

PROMPT_TT = 64
FFN_TM = 512

F32 = jnp.float32
BF16 = jnp.bfloat16
_TRANS_B = (((1,), (1,)), ((), ()))


def _dot(a, b):
    return jnp.dot(a, b, preferred_element_type=F32)


def _dot_tb(a, b):
    return lax.dot_general(a, b, _TRANS_B, preferred_element_type=F32)


def _layer_norm(y, g, b):
    mu = jnp.mean(y, axis=-1, keepdims=True)
    d = y - mu
    var = jnp.mean(d * d, axis=-1, keepdims=True)
    return d * lax.rsqrt(var + LN_EPS) * g + b


def _rope(x, cos, sin_signed):
    lane = lax.broadcasted_iota(jnp.int32, (x.shape[0], LANES), 1)
    first_half = (lane & (HEAD_DIM - 1)) < (HEAD_DIM // 2)
    outs = []
    for g in range(x.shape[1] // LANES):
        xg = x[:, g * LANES:(g + 1) * LANES]
        partner = jnp.where(first_half,
                            pltpu.roll(xg, LANES - HEAD_DIM // 2, 1),
                            pltpu.roll(xg, HEAD_DIM // 2, 1))
        outs.append(xg * cos + partner * sin_signed)
    return outs[0] if len(outs) == 1 else jnp.concatenate(outs, axis=1)


def _softmax_pv(s, valid, sink_col, vv):
    s = jnp.where(valid, s, NEG_INF)
    m = jnp.maximum(jnp.max(s, axis=-1, keepdims=True), sink_col)
    p = jnp.exp(s - m)
    denom = jnp.sum(p, axis=-1, keepdims=True) + jnp.exp(sink_col - m)
    return _dot(p.astype(BF16), vv) * (1.0 / denom)


def _sink_column(sink_ref, layer, kv, tt):
    row = lax.broadcasted_iota(jnp.int32, (Q_PER_KV * tt, 1), 0)
    col = jnp.full((Q_PER_KV * tt, 1), sink_ref[layer, Q_PER_KV * kv + Q_PER_KV - 1], F32)
    for g in range(Q_PER_KV - 2, -1, -1):
        col = jnp.where(row < (g + 1) * tt, sink_ref[layer, Q_PER_KV * kv + g], col)
    return col


def _window_valid(tt, w, jmin):
    t_idx = lax.broadcasted_iota(jnp.int32, (Q_PER_KV * tt, w), 0) & (tt - 1)
    j_idx = lax.broadcasted_iota(jnp.int32, (Q_PER_KV * tt, w), 1)
    return (j_idx > t_idx) & (j_idx <= t_idx + WINDOW) & (j_idx >= jmin)


def _recurrent_branch(xr, gate, xe_ref, a_ref, b_ref, hs_ref, h_ref,
                      convw_ref, convb_ref, wax_ref, ba_ref, bx_ref, lam_ref, *, ns, tt):
    r_rows = tt * ns
    hist = (CONV_W - 1) * ns
    xe_ref[hist:hist + r_rows, :] = xr
    xc = convb_ref[...] + xe_ref[0:r_rows, :] * convw_ref[0:1, :]
    for j in range(1, CONV_W):
        xc = xc + xe_ref[j * ns:j * ns + r_rows, :] * convw_ref[j:j + 1, :]
    new_hist = xe_ref[r_rows:r_rows + hist, :]
    xe_ref[0:hist, :] = new_hist

    xcb = xc.astype(BF16)
    r_parts, i_parts = [], []
    for n in range(N_LRU_BLOCKS):
        ri = _dot(xcb[:, n * LRU_BLOCK:(n + 1) * LRU_BLOCK], wax_ref[n])
        r_parts.append(ri[:, :LRU_BLOCK])
        i_parts.append(ri[:, LRU_BLOCK:])
    r_gate = jax.nn.sigmoid(jnp.concatenate(r_parts, axis=1) + ba_ref[...])
    i_gate = jax.nn.sigmoid(jnp.concatenate(i_parts, axis=1) + bx_ref[...])
    log_a = (-LRU_C) * r_gate * jax.nn.softplus(-lam_ref[...])
    a = jnp.exp(log_a)
    a_ref[...] = a
    y = -jnp.tanh(log_a) * (a * a + 1.0)
    b_ref[...] = jnp.where(y > 0.0, y * lax.rsqrt(y), 0.0) * (i_gate * xc)

    h = h_ref[...]
    for t in range(tt):
        h = a_ref[t * ns:(t + 1) * ns, :] * h + b_ref[t * ns:(t + 1) * ns, :]
        hs_ref[t * ns:(t + 1) * ns, :] = h
    h_ref[...] = h
    return hs_ref[...] * jax.nn.gelu(gate)


def _merge_norm(x, gates, attn, rec, wap_ref, wlp_ref, wout_ref, g1_ref, b1_ref, alpha):
    ap = _dot(attn.astype(BF16), wap_ref[...])
    lp = _dot(rec.astype(BF16), wlp_ref[...])
    merged = (jax.nn.sigmoid(gates[:, :D_MODEL]) * ap
              + jax.nn.sigmoid(gates[:, D_MODEL:]) * lp)
    mix = _dot(merged.astype(BF16), wout_ref[...])
    return _layer_norm(alpha * x + mix, g1_ref[...], b1_ref[...])


def _layer_spec(arr, layer):
    nd = arr.ndim
    return pl.BlockSpec((None,) + tuple(arr.shape[1:]), lambda *_: (layer,) + (0,) * (nd - 1),
                        pipeline_mode=pl.Buffered(1))


def _full_spec(shape):
    nd = len(shape)
    return pl.BlockSpec(tuple(shape), lambda *_: (0,) * nd, pipeline_mode=pl.Buffered(1))


_COMPILER_PARAMS = pltpu.CompilerParams(dimension_semantics=("arbitrary",),
                                        vmem_limit_bytes=VMEM_LIMIT_BYTES)


def _mixer_tile(load_x, cos, sin, y_ref, pos_start, sink_ref, win_ref, wap_ref, wlp_ref, wout_ref,
                convw_ref, convb_ref, wax_ref, ba_ref, bx_ref, lam_ref, g1_ref, b1_ref,
                q_s, k_s, v_s, attn_s, kwin_s, vwin_s, xe_s, a_s, b_s, hs_s, h_s, *, layer, ns, tt, alpha):
    r_rows = ns * tt
    n_qg = ATTN_WIDTH // LANES
    proj = _dot(load_x().astype(BF16), win_ref[...])
    qkv = proj[:, 0:V_END]
    cos_q = cos * (HEAD_DIM ** -0.5)
    sin_q = sin * (HEAD_DIM ** -0.5)
    for j in range(n_qg):
        q_s[j, 0:r_rows, :] = _rope(qkv[:, j * LANES:(j + 1) * LANES], cos_q, sin_q)
    k_s[0:r_rows, :] = _rope(qkv[:, Q_END:K_END], cos, sin)
    v_s[0:r_rows, :] = qkv[:, K_END:V_END]

    valid = _window_valid(tt, WINDOW + tt, WINDOW - pos_start)
    sink_cols = [_sink_column(sink_ref, layer, kv, tt) for kv in range(N_KV_HEADS)]
    for b in range(ns):
        rows = pl.ds(b, tt, stride=ns)
        kwin_s[b, WINDOW:WINDOW + tt, :] = k_s[rows, :]
        vwin_s[b, WINDOW:WINDOW + tt, :] = v_s[rows, :]
        keys = kwin_s[b, 0:WINDOW + tt, :]
        vals = vwin_s[b, 0:WINDOW + tt, :]
        kb = keys.astype(BF16)
        vb = vals.astype(BF16)
        q_b = [q_s[j, rows, :].astype(BF16) for j in range(n_qg)]
        heads = [q_b[h // 2][:, (h % 2) * HEAD_DIM:(h % 2 + 1) * HEAD_DIM] for h in range(N_Q_HEADS)]
        outs = []
        for kv in range(N_KV_HEADS):
            kk = kb[:, kv * HEAD_DIM:(kv + 1) * HEAD_DIM]
            vv = vb[:, kv * HEAD_DIM:(kv + 1) * HEAD_DIM]
            qs = jnp.concatenate(heads[Q_PER_KV * kv:Q_PER_KV * (kv + 1)], axis=0)
            o = _softmax_pv(_dot_tb(qs, kk), valid, sink_cols[kv], vv)
            outs.extend(o[g * tt:(g + 1) * tt] for g in range(Q_PER_KV))
        for j in range(n_qg):
            attn_s[j, rows, :] = jnp.concatenate(outs[2 * j:2 * j + 2], axis=1)
        kwin_s[b, 0:WINDOW, :] = keys[tt:tt + WINDOW]
        vwin_s[b, 0:WINDOW, :] = vals[tt:tt + WINDOW]

    tile_rows = pl.ds(0, r_rows)
    rec = _recurrent_branch(proj[:, V_END:XR_END], proj[:, XR_END:GATE_END], xe_s,
                            a_s.at[tile_rows], b_s.at[tile_rows], hs_s.at[tile_rows], h_s,
                            convw_ref, convb_ref, wax_ref, ba_ref, bx_ref, lam_ref, ns=ns, tt=tt)
    attn = jnp.concatenate([attn_s[j, 0:r_rows, :] for j in range(n_qg)], axis=1)
    y_ref[...] = _merge_norm(load_x(), proj[:, GATE_END:D_IN], attn, rec, wap_ref, wlp_ref, wout_ref,
                             g1_ref, b1_ref, alpha)


def _mixer_kernel(sink_ref, xm_ref, x_ref, cosm_ref, sinm_ref, cos_ref, sin_ref,
                  win_ref, wap_ref, wlp_ref, wout_ref,
                  convw_ref, convb_ref, wax_ref, ba_ref, bx_ref, lam_ref, g1_ref, b1_ref,
                  ym_ref, y_ref, kwin_out_ref, vwin_out_ref, conv_out_ref, h_out_ref,
                  q_s, k_s, v_s, attn_s, kwin_s, vwin_s, xe_s, a_s, b_s, hs_s, h_s, *maybe_xt_s,
                  layer, ns, tt, tt_meta, alpha):
    step = pl.program_id(0)
    hist = (CONV_W - 1) * ns
    n_slabs = D_MODEL // LANES
    shared = (sink_ref, win_ref, wap_ref, wlp_ref, wout_ref,
              convw_ref, convb_ref, wax_ref, ba_ref, bx_ref, lam_ref, g1_ref, b1_ref,
              q_s, k_s, v_s, attn_s, kwin_s, vwin_s, xe_s, a_s, b_s, hs_s, h_s)

    @pl.when(step == 0)
    def _():
        kwin_s[:, 0:WINDOW, :] = jnp.zeros((ns, WINDOW, KV_WIDTH), F32)
        vwin_s[:, 0:WINDOW, :] = jnp.zeros((ns, WINDOW, KV_WIDTH), F32)
        xe_s[0:hist, :] = jnp.zeros((hist, D_RNN), F32)
        h_s[...] = jnp.zeros((ns, D_RNN), F32)
        _mixer_tile(lambda: xm_ref[...], cosm_ref[...], sinm_ref[...], ym_ref, 0, *shared,
                    layer=layer, ns=ns, tt=tt_meta, alpha=alpha)

    @pl.when(step > 0)
    def _():
        if maybe_xt_s:
            xt_s, = maybe_xt_s
            for b in range(ns):
                for c in range(n_slabs):
                    xt_s[c, pl.ds(b, tt, stride=ns), :] = x_ref[b, :, c * LANES:(c + 1) * LANES]
            load_x = lambda: jnp.concatenate([xt_s[c] for c in range(n_slabs)], axis=1)
        else:
            load_x = lambda: x_ref[...]
        _mixer_tile(load_x, cos_ref[...], sin_ref[...], y_ref, tt_meta + (step - 1) * tt, *shared,
                    layer=layer, ns=ns, tt=tt, alpha=alpha)

    @pl.when(step == pl.num_programs(0) - 1)
    def _():
        kwin_out_ref[...] = kwin_s[:, 0:WINDOW, :]
        vwin_out_ref[...] = vwin_s[:, 0:WINDOW, :]
        conv_out_ref[...] = xe_s[0:hist, :]
        h_out_ref[...] = h_s[...]


def _mixer_call(xm, x, tables_m, tables_p, wts, *, layer, ns, tt, alpha):
    batch_major = x.ndim == 3
    rows = x.shape[0] * x.shape[1] if batch_major else x.shape[0]
    r_rows = ns * tt
    n_tiles = rows // r_rows
    rows_m = xm.shape[0]
    tt_meta = rows_m // ns
    hist = (CONV_W - 1) * ns

    def row_spec(c):
        return pl.BlockSpec((r_rows, c), lambda i: (jnp.maximum(i - 1, 0), 0))

    if batch_major:
        x_spec = pl.BlockSpec((ns, tt, D_MODEL), lambda i: (0, jnp.maximum(i - 1, 0), 0))
    else:
        x_spec = row_spec(D_MODEL)
    weights = [wts[k] for k in ('w_in', 'w_attn_proj', 'w_lru_proj', 'w_out', 'conv_w', 'conv_b', 'w_ax',
                                'lru_ba', 'lru_bx', 'lru_lambda', 'ln1_g', 'ln1_b')]
    in_specs = ([pl.BlockSpec(memory_space=pltpu.SMEM), _full_spec(xm.shape), x_spec,
                 _full_spec(tables_m[0].shape), _full_spec(tables_m[1].shape), row_spec(LANES), row_spec(LANES)]
                + [_layer_spec(w, layer) for w in weights])
    out_shape = (jax.ShapeDtypeStruct((rows_m, D_MODEL), F32),
                 jax.ShapeDtypeStruct((rows, D_MODEL), F32),
                 jax.ShapeDtypeStruct((ns, WINDOW, KV_WIDTH), F32),
                 jax.ShapeDtypeStruct((ns, WINDOW, KV_WIDTH), F32),
                 jax.ShapeDtypeStruct((hist, D_RNN), F32),
                 jax.ShapeDtypeStruct((ns, D_RNN), F32))
    out_specs = (pl.BlockSpec((rows_m, D_MODEL), lambda i: (0, 0)),
                 row_spec(D_MODEL),
                 pl.BlockSpec((ns, WINDOW, KV_WIDTH), lambda i: (0, 0, 0)),
                 pl.BlockSpec((ns, WINDOW, KV_WIDTH), lambda i: (0, 0, 0)),
                 pl.BlockSpec((hist, D_RNN), lambda i: (0, 0)),
                 pl.BlockSpec((ns, D_RNN), lambda i: (0, 0)))
    scratch = [pltpu.VMEM((ATTN_WIDTH // LANES, r_rows, LANES), F32),
               pltpu.VMEM((r_rows, KV_WIDTH), F32),
               pltpu.VMEM((r_rows, KV_WIDTH), F32),
               pltpu.VMEM((ATTN_WIDTH // LANES, r_rows, LANES), F32),
               pltpu.VMEM((ns, WINDOW + tt, KV_WIDTH), F32),
               pltpu.VMEM((ns, WINDOW + tt, KV_WIDTH), F32),
               pltpu.VMEM((hist + r_rows, D_RNN), F32),
               pltpu.VMEM((r_rows, D_RNN), F32),
               pltpu.VMEM((r_rows, D_RNN), F32),
               pltpu.VMEM((r_rows, D_RNN), F32),
               pltpu.VMEM((ns, D_RNN), F32)]
    if batch_major:
        scratch.append(pltpu.VMEM((D_MODEL // LANES, r_rows, LANES), F32))
    outs = pl.pallas_call(
        functools.partial(_mixer_kernel, layer=layer, ns=ns, tt=tt, tt_meta=tt_meta, alpha=alpha),
        grid=(n_tiles + 1,),
        in_specs=in_specs,
        out_specs=out_specs,
        out_shape=out_shape,
        scratch_shapes=scratch,
        compiler_params=_COMPILER_PARAMS,
        name=f"mixer_ns{ns}_tt{tt}",
    )(wts['sinks'], xm, x, *tables_m, *tables_p, *weights)
    return outs[0], outs[1], outs[2:]


def _ffn_tile(x, win_ref, wout_ref, g_ref, b_ref, alpha):
    u = _dot(x.astype(BF16), win_ref[...])
    hmid = jax.nn.silu(u[:, :D_FF]) * u[:, D_FF:]
    f = _dot(hmid.astype(BF16), wout_ref[...])
    return _layer_norm(alpha * x + f, g_ref[...], b_ref[...])


def _ffn_kernel(x_hbm, xa_ref, xb_ref, win_ref, wout_ref, g_ref, b_ref, y_hbm, ya_ref, yb_ref,
                *maybe_yt_s, alpha, tm, batch_major_ns):
    def tile_body(x_ref, y_ref):
        y = _ffn_tile(x_ref[...], win_ref, wout_ref, g_ref, b_ref, alpha)
        if maybe_yt_s:
            yt_s, = maybe_yt_s
            ns, tt, _ = y_ref.shape
            for c in range(D_MODEL // LANES):
                yt_s[c] = y[:, c * LANES:(c + 1) * LANES]
            for b in range(ns):
                for c in range(D_MODEL // LANES):
                    y_ref[b, :, c * LANES:(c + 1) * LANES] = yt_s[c, pl.ds(b, tt, stride=ns), :]
        else:
            y_ref[...] = y

    n_main = x_hbm.shape[0] // tm
    if batch_major_ns is None:
        out_spec = pl.BlockSpec((tm, D_MODEL), lambda i: (i, 0))
    else:
        out_spec = pl.BlockSpec((batch_major_ns, tm // batch_major_ns, D_MODEL), lambda i: (0, i, 0))
    pltpu.emit_pipeline(tile_body, grid=(n_main,),
                        in_specs=[pl.BlockSpec((tm, D_MODEL), lambda i: (i, 0))],
                        out_specs=[out_spec])(x_hbm, y_hbm)
    ya_ref[...] = _ffn_tile(xa_ref[...], win_ref, wout_ref, g_ref, b_ref, alpha)
    yb_ref[...] = _ffn_tile(xb_ref[...], win_ref, wout_ref, g_ref, b_ref, alpha)


def _ffn_call(x, x_a, x_b, wts, *, layer, tm, alpha, batch_major_out=None):
    rows = x.shape[0]
    weights = [wts[k] for k in ('w_ffn_in', 'w_ffn_out', 'ln2_g', 'ln2_b')]
    any_spec = pl.BlockSpec(memory_space=pl.ANY)
    if batch_major_out is None:
        out_shape = jax.ShapeDtypeStruct((rows, D_MODEL), F32)
        scratch = []
    else:
        ns = batch_major_out
        out_shape = jax.ShapeDtypeStruct((ns, rows // ns, D_MODEL), F32)
        scratch = [pltpu.VMEM((D_MODEL // LANES, tm, LANES), F32)]
    return pl.pallas_call(
        functools.partial(_ffn_kernel, alpha=alpha, tm=tm, batch_major_ns=batch_major_out),
        grid=(1,),
        in_specs=([any_spec, _full_spec(x_a.shape), _full_spec(x_b.shape)]
                  + [_layer_spec(w, layer) for w in weights]),
        out_specs=(any_spec, pl.BlockSpec(x_a.shape, lambda i: (0, 0)), pl.BlockSpec(x_b.shape, lambda i: (0, 0))),
        out_shape=(out_shape, jax.ShapeDtypeStruct(x_a.shape, F32), jax.ShapeDtypeStruct(x_b.shape, F32)),
        scratch_shapes=scratch,
        compiler_params=_COMPILER_PARAMS,
        name=f"ffn_tm{tm}",
    )(x, x_a, x_b, *weights)


def _sample_pre_kernel(x_ref, cos_ref, sin_ref, win_ref,
                       convw_ref, convb_ref, wax_ref, ba_ref, bx_ref, lam_ref,
                       conv0_ref, h0_ref,
                       q_ref, k_ref, v_ref, rec_ref, conv_out_ref, h_out_ref,
                       xe_s, a_s, b_s, hs_s, h_s, *, ns, tt):
    hist = (CONV_W - 1) * ns
    xe_s[0:hist, :] = conv0_ref[...]
    h_s[...] = h0_ref[...]
    xb = x_ref[...].astype(BF16)
    qkv = _dot(xb, win_ref[:, 0:V_END])
    cos = cos_ref[...]
    sin = sin_ref[...]
    q_ref[...] = _rope(qkv[:, :Q_END], cos, sin)
    k_ref[...] = _rope(qkv[:, Q_END:K_END], cos, sin)
    v_ref[...] = qkv[:, K_END:V_END]
    xg = _dot(xb, win_ref[:, V_END:GATE_END])
    rec_ref[...] = _recurrent_branch(xg[:, :D_RNN], xg[:, D_RNN:], xe_s, a_s, b_s, hs_s, h_s,
                                     convw_ref, convb_ref, wax_ref, ba_ref, bx_ref, lam_ref,
                                     ns=ns, tt=tt)
    conv_out_ref[...] = xe_s[0:hist, :]
    h_out_ref[...] = h_s[...]


def _sample_pre_call(x, cos, sin, wts, conv_all, h_all, *, layer, ns, tt):
    rows = ns * tt
    hist = (CONV_W - 1) * ns
    weights = [wts[k] for k in ('w_in', 'conv_w', 'conv_b', 'w_ax', 'lru_ba', 'lru_bx', 'lru_lambda')]
    out_shape = (jax.ShapeDtypeStruct((rows, ATTN_WIDTH), F32),
                 jax.ShapeDtypeStruct((rows, KV_WIDTH), F32),
                 jax.ShapeDtypeStruct((rows, KV_WIDTH), F32),
                 jax.ShapeDtypeStruct((rows, D_RNN), F32),
                 jax.ShapeDtypeStruct((hist, D_RNN), F32),
                 jax.ShapeDtypeStruct((ns, D_RNN), F32))
    return pl.pallas_call(
        functools.partial(_sample_pre_kernel, ns=ns, tt=tt),
        grid=(1,),
        in_specs=([_full_spec(a.shape) for a in (x, cos, sin)]
                  + [_layer_spec(w, layer) for w in weights]
                  + [_layer_spec(conv_all, layer), _layer_spec(h_all, layer)]),
        out_specs=tuple(pl.BlockSpec(s.shape, lambda i, _n=len(s.shape): (0,) * _n) for s in out_shape),
        out_shape=out_shape,
        scratch_shapes=[pltpu.VMEM((hist + rows, D_RNN), F32),
                        pltpu.VMEM((rows, D_RNN), F32),
                        pltpu.VMEM((rows, D_RNN), F32),
                        pltpu.VMEM((rows, D_RNN), F32),
                        pltpu.VMEM((ns, D_RNN), F32)],
        compiler_params=_COMPILER_PARAMS,
        name="sample_pre",
    )(x, cos, sin, *weights, conv_all, h_all)


def _sample_attn_kernel(sink_ref, q_ref, knt_ref, vnt_ref, kc_ref, vc_ref,
                        kprev_ref, vprev_ref, o_ref, ko_ref, vo_ref, *, layer, nb, tt):
    del kprev_ref, vprev_ref
    rows = N_Q_HEADS * tt
    scale = HEAD_DIM ** -0.5
    t_row = lax.broadcasted_iota(jnp.int32, (rows, 2 * WINDOW), 0) & (tt - 1)
    col = lax.broadcasted_iota(jnp.int32, (rows, 2 * WINDOW), 1)
    valid_old = (col < WINDOW) & (col > t_row) & (col >= WINDOW - PAST_LEN)
    lane = lax.broadcasted_iota(jnp.int32, (KV_WIDTH, WINDOW), 1)
    keep_old = lane < WINDOW - tt
    row1 = lax.broadcasted_iota(jnp.int32, (rows, 1), 0)
    sink_col = jnp.full((rows, 1), sink_ref[layer, N_Q_HEADS - 1], F32)
    for h in range(N_Q_HEADS - 2, -1, -1):
        sink_col = jnp.where(row1 < (h + 1) * tt, sink_ref[layer, h], sink_col)
    knt = knt_ref[...]
    vnt = vnt_ref[...]
    knt_b = knt.astype(BF16)
    vnt_b = vnt.astype(BF16)
    for s in range(nb):
        k_old = kc_ref[s]
        v_old = vc_ref[s]
        lo = WINDOW + s * tt
        valid = valid_old | ((col >= lo) & (col <= lo + t_row))
        k_all = jnp.concatenate([k_old.astype(BF16), knt_b], axis=1)
        v_all = jnp.concatenate([v_old.astype(BF16), vnt_b], axis=1)
        sc = jnp.where(valid, _dot(q_ref[s].astype(BF16), k_all) * scale, NEG_INF)
        m = jnp.maximum(jnp.max(sc, axis=-1, keepdims=True), sink_col)
        p = jnp.exp(sc - m)
        denom = jnp.sum(p, axis=-1, keepdims=True) + jnp.exp(sink_col - m)
        o_ref[s] = _dot_tb(p.astype(BF16), v_all) * (1.0 / denom)
        new_shift = (WINDOW - tt - tt * s) % WINDOW
        ko_ref[s] = jnp.where(keep_old, pltpu.roll(k_old, WINDOW - tt, 1), pltpu.roll(knt, new_shift, 1))
        vo_ref[s] = jnp.where(keep_old, pltpu.roll(v_old, WINDOW - tt, 1), pltpu.roll(vnt, new_shift, 1))


def _sample_attn_call(sinks, qz, knt, vnt, kc_all, vc_all, kprev, vprev, *, layer, tt):
    nseq = qz.shape[0]
    nb = WINDOW // tt

    q_spec = pl.BlockSpec((nb,) + tuple(qz.shape[1:]), lambda i: (i, 0, 0))
    cache_spec = pl.BlockSpec((None, nb, KV_WIDTH, WINDOW), lambda i: (layer, i, 0, 0))
    col_spec = pl.BlockSpec((KV_WIDTH, WINDOW), lambda i: (0, i))
    any_spec = pl.BlockSpec(memory_space=pl.ANY)
    out_shape = (jax.ShapeDtypeStruct(qz.shape, F32),
                 jax.ShapeDtypeStruct(kprev.shape, F32),
                 jax.ShapeDtypeStruct(vprev.shape, F32))
    return pl.pallas_call(
        functools.partial(_sample_attn_kernel, layer=layer, nb=nb, tt=tt),
        grid=(nseq // nb,),
        in_specs=[pl.BlockSpec(memory_space=pltpu.SMEM), q_spec, col_spec, col_spec,
                  cache_spec, cache_spec, any_spec, any_spec],
        out_specs=(q_spec, cache_spec, cache_spec),
        out_shape=out_shape,
        input_output_aliases={6: 1, 7: 2},
        compiler_params=_COMPILER_PARAMS,
        name="sample_attn",
    )(sinks, qz, knt, vnt, kc_all, vc_all, kprev, vprev)


def _sample_post_kernel(x_ref, attn_ref, rec_ref, win_ref, wap_ref, wlp_ref, wout_ref,
                        g1_ref, b1_ref, y_ref, *, alpha):
    x = x_ref[...]
    gates = _dot(x.astype(BF16), win_ref[:, GATE_END:D_IN])
    y_ref[...] = _merge_norm(x, gates, attn_ref[...], rec_ref[...],
                             wap_ref, wlp_ref, wout_ref, g1_ref, b1_ref, alpha)


def _sample_post_call(x, attn, rec, wts, *, layer, alpha):
    weights = [wts[k] for k in ('w_in', 'w_attn_proj', 'w_lru_proj', 'w_out', 'ln1_g', 'ln1_b')]
    return pl.pallas_call(
        functools.partial(_sample_post_kernel, alpha=alpha),
        grid=(1,),
        in_specs=[_full_spec(a.shape) for a in (x, attn, rec)] + [_layer_spec(w, layer) for w in weights],
        out_specs=pl.BlockSpec(x.shape, lambda i: (0, 0)),
        out_shape=jax.ShapeDtypeStruct(x.shape, F32),
        compiler_params=_COMPILER_PARAMS,
        name="sample_post",
    )(x, attn, rec, *weights)


def _rope_tables(pos, repeat):
    half = HEAD_DIM // 2
    inv = ROPE_THETA ** (-jnp.arange(half, dtype=F32) / half)
    ang = pos.astype(F32)[:, None] * inv[None, :]
    cos = jnp.cos(ang)
    sin = jnp.sin(ang)
    cos_t = jnp.tile(cos, (1, LANES // half))
    sin_t = jnp.tile(jnp.concatenate([-sin, sin], axis=1), (1, LANES // HEAD_DIM))
    return jnp.repeat(cos_t, repeat, axis=0), jnp.repeat(sin_t, repeat, axis=0)


def _prepare_weights(w_in, w_attn_proj, w_lru_proj, w_out, attn_sinks, conv_w, conv_b,
                     lru_wa, lru_ba, lru_wx, lru_bx, lru_lambda, ln1_g, ln1_b,
                     w_ffn_in, w_ffn_out, ln2_g, ln2_b):
    depth = w_in.shape[0]
    row = lambda v: v.reshape(depth, 1, -1).astype(F32)
    return dict(
        w_in=w_in.astype(BF16),
        w_attn_proj=w_attn_proj.astype(BF16),
        w_lru_proj=w_lru_proj.astype(BF16),
        w_out=w_out.astype(BF16),
        sinks=attn_sinks.astype(F32),
        conv_w=conv_w.astype(F32),
        conv_b=row(conv_b),
        w_ax=jnp.concatenate([lru_wa, lru_wx], axis=-1).astype(BF16),
        lru_ba=row(lru_ba), lru_bx=row(lru_bx), lru_lambda=row(lru_lambda),
        ln1_g=row(ln1_g), ln1_b=row(ln1_b),
        w_ffn_in=w_ffn_in.astype(BF16),
        w_ffn_out=w_ffn_out.astype(BF16),
        ln2_g=row(ln2_g), ln2_b=row(ln2_b),
    )


def kernel(x_prompt, x_sample, cache_win_k, cache_win_v, state_conv, state_lru, meta_tokens, w_in, w_attn_proj, w_lru_proj, w_out, attn_sinks, conv_w, conv_b, lru_wa, lru_ba, lru_wx, lru_bx, lru_lambda, ln1_g, ln1_b, w_ffn_in, w_ffn_out, ln2_g, ln2_b):
    depth = w_in.shape[0]
    alpha = (2 * depth) ** 0.25
    batch, seq, d = x_prompt.shape
    dec_batch, dec_seq, _ = x_sample.shape
    wts = _prepare_weights(w_in, w_attn_proj, w_lru_proj, w_out, attn_sinks, conv_w, conv_b,
                           lru_wa, lru_ba, lru_wx, lru_bx, lru_lambda, ln1_g, ln1_b,
                           w_ffn_in, w_ffn_out, ln2_g, ln2_b)

    xp = x_prompt
    xm = jnp.broadcast_to(meta_tokens.astype(F32)[:, None, :], (N_META, batch, d)).reshape(N_META * batch, d)
    xs = jnp.transpose(x_sample, (1, 0, 2)).reshape(dec_seq * dec_batch, d)

    cos_m, sin_m = _rope_tables(jnp.arange(N_META, dtype=jnp.int32), batch)
    cos_p, sin_p = _rope_tables(N_META + jnp.arange(seq, dtype=jnp.int32), batch)
    cos_s, sin_s = _rope_tables(PAST_LEN + jnp.arange(dec_seq, dtype=jnp.int32), dec_batch)

    conv_all = jnp.transpose(state_conv, (0, 2, 1, 3)).reshape(depth, (CONV_W - 1) * dec_batch, D_RNN)
    kc_all = jnp.transpose(cache_win_k, (0, 1, 3, 4, 2)).reshape(depth, dec_batch, KV_WIDTH, WINDOW)
    vc_all = jnp.transpose(cache_win_v, (0, 1, 3, 4, 2)).reshape(depth, dec_batch, KV_WIDTH, WINDOW)
    win_k_s = jnp.zeros(kc_all.shape, F32)
    win_v_s = jnp.zeros(vc_all.shape, F32)

    outs = [[] for _ in range(6)]
    for l in range(depth):
        xm1, xp1, (kw, vw, cv, hh) = _mixer_call(xm, xp, (cos_m, sin_m), (cos_p, sin_p), wts,
                                                  layer=l, ns=batch, tt=PROMPT_TT, alpha=alpha)
        outs[0].append(kw.reshape(batch, WINDOW, N_KV_HEADS, HEAD_DIM))
        outs[1].append(vw.reshape(batch, WINDOW, N_KV_HEADS, HEAD_DIM))
        outs[2].append(jnp.transpose(cv.reshape(CONV_W - 1, batch, D_RNN), (1, 0, 2)))
        outs[3].append(hh)

        q, k, v, rec, cv_s, hh_s = _sample_pre_call(xs, cos_s, sin_s, wts, conv_all, state_lru,
                                                    layer=l, ns=dec_batch, tt=dec_seq)
        q5 = jnp.transpose(q.reshape(dec_seq, dec_batch, N_KV_HEADS, Q_PER_KV, HEAD_DIM), (1, 2, 3, 0, 4))
        own_group = jnp.eye(N_KV_HEADS, dtype=bool)[None, :, None, None, :, None]
        qz = jnp.where(own_group, q5[:, :, :, :, None, :], 0.0).reshape(dec_batch, N_Q_HEADS * dec_seq, KV_WIDTH)
        knt = jnp.transpose(k.reshape(dec_seq, dec_batch, KV_WIDTH), (2, 1, 0)).reshape(KV_WIDTH, dec_batch * dec_seq)
        vnt = jnp.transpose(v.reshape(dec_seq, dec_batch, KV_WIDTH), (2, 1, 0)).reshape(KV_WIDTH, dec_batch * dec_seq)
        oz, win_k_s, win_v_s = _sample_attn_call(wts['sinks'], qz, knt, vnt, kc_all, vc_all,
                                                 win_k_s, win_v_s, layer=l, tt=dec_seq)
        o6 = oz.reshape(dec_batch, N_KV_HEADS, Q_PER_KV, dec_seq, N_KV_HEADS, HEAD_DIM)
        o5 = jnp.stack([o6[:, kv, :, :, kv, :] for kv in range(N_KV_HEADS)], axis=1)
        attn = jnp.transpose(o5, (3, 0, 1, 2, 4)).reshape(dec_seq * dec_batch, ATTN_WIDTH)
        xs1 = _sample_post_call(xs, attn, rec, wts, layer=l, alpha=alpha)
        xp, xs, xm = _ffn_call(xp1, xs1, xm1, wts, layer=l, tm=FFN_TM, alpha=alpha,
                               batch_major_out=batch if l == depth - 1 else None)
        outs[4].append(cv_s.reshape(CONV_W - 1, dec_batch, D_RNN))
        outs[5].append(hh_s)

    y_prompt = xp
    y_sample = jnp.transpose(xs.reshape(dec_seq, dec_batch, d), (1, 0, 2))
    st = [jnp.stack(o) for o in outs]
    win_shape = (depth, dec_batch, N_KV_HEADS, HEAD_DIM, WINDOW)
    win_k_sample = jnp.transpose(win_k_s.reshape(win_shape), (0, 1, 4, 2, 3))
    win_v_sample = jnp.transpose(win_v_s.reshape(win_shape), (0, 1, 4, 2, 3))
    conv_sample = jnp.transpose(st[4], (0, 2, 1, 3))
    return (y_prompt, y_sample, st[0], st[1], st[2], st[3], win_k_sample, win_v_sample, conv_sample, st[5])
```

```python
import functools

import jax
import jax.numpy as jnp
from jax import lax
from jax.experimental import pallas as pl
from jax.experimental.pallas import tpu as pltpu

D_MODEL = 1024
N_META = 16
HEAD_DIM = 64
N_Q_HEADS = 8
N_KV_HEADS = 2
Q_PER_KV = N_Q_HEADS // N_KV_HEADS
ATTN_WIDTH = N_Q_HEADS * HEAD_DIM
KV_WIDTH = N_KV_HEADS * HEAD_DIM
WINDOW = 128
ROPE_THETA = 10000.0
D_RNN = D_MODEL
N_LRU_BLOCKS = 8
LRU_BLOCK = D_RNN // N_LRU_BLOCKS
CONV_W = 4
LRU_C = 8.0
D_FF = -(-8 * D_MODEL // (3 * 256)) * 256
LN_EPS = 1e-5
NEG_INF = -1e30
PAST_LEN = 8192

Q_END = ATTN_WIDTH
K_END = Q_END + KV_WIDTH
V_END = K_END + KV_WIDTH
XR_END = V_END + D_RNN
GATE_END = XR_END + D_RNN
D_IN = GATE_END + 2 * D_MODEL

LANES = 128
VMEM_LIMIT_BYTES = 56 * 1024 * 1024

PROMPT_TT = 64
FFN_TM = 512

F32 = jnp.float32
BF16 = jnp.bfloat16
_TRANS_B = (((1,), (1,)), ((), ()))


def _dot(a, b):
    return jnp.dot(a, b, preferred_element_type=F32)


def _dot_tb(a, b):
    return lax.dot_general(a, b, _TRANS_B, preferred_element_type=F32)


def _layer_norm(y, g, b):
    mu = jnp.mean(y, axis=-1, keepdims=True)
    d = y - mu
    var = jnp.mean(d * d, axis=-1, keepdims=True)
    return d * lax.rsqrt(var + LN_EPS) * g + b


def _rope(x, cos, sin_signed):
    lane = lax.broadcasted_iota(jnp.int32, (x.shape[0], LANES), 1)
    first_half = (lane & (HEAD_DIM - 1)) < (HEAD_DIM // 2)
    outs = []
    for g in range(x.shape[1] // LANES):
        xg = x[:, g * LANES:(g + 1) * LANES]
        partner = jnp.where(first_half,
                            pltpu.roll(xg, LANES - HEAD_DIM // 2, 1),
                            pltpu.roll(xg, HEAD_DIM // 2, 1))
        outs.append(xg * cos + partner * sin_signed)
    return outs[0] if len(outs) == 1 else jnp.concatenate(outs, axis=1)


def _softmax_pv(s, valid, sink_col, vv):
    s = jnp.where(valid, s, NEG_INF)
    m = jnp.maximum(jnp.max(s, axis=-1, keepdims=True), sink_col)
    p = jnp.exp(s - m)
    denom = jnp.sum(p, axis=-1, keepdims=True) + jnp.exp(sink_col - m)
    return _dot(p.astype(BF16), vv) * (1.0 / denom)


def _sink_column(sink_ref, layer, kv, tt):
    row = lax.broadcasted_iota(jnp.int32, (Q_PER_KV * tt, 1), 0)
    col = jnp.full((Q_PER_KV * tt, 1), sink_ref[layer, Q_PER_KV * kv + Q_PER_KV - 1], F32)
    for g in range(Q_PER_KV - 2, -1, -1):
        col = jnp.where(row < (g + 1) * tt, sink_ref[layer, Q_PER_KV * kv + g], col)
    return col


def _window_valid(tt, w, jmin):
    t_idx = lax.broadcasted_iota(jnp.int32, (Q_PER_KV * tt, w), 0) & (tt - 1)
    j_idx = lax.broadcasted_iota(jnp.int32, (Q_PER_KV * tt, w), 1)
    return (j_idx > t_idx) & (j_idx <= t_idx + WINDOW) & (j_idx >= jmin)


def _recurrent_branch(xr, gate, xe_ref, a_ref, b_ref, hs_ref, h_ref,
                      convw_ref, convb_ref, wax_ref, ba_ref, bx_ref, lam_ref, *, ns, tt):
    r_rows = tt * ns
    hist = (CONV_W - 1) * ns
    xe_ref[hist:hist + r_rows, :] = xr
    xc = convb_ref[...] + xe_ref[0:r_rows, :] * convw_ref[0:1, :]
    for j in range(1, CONV_W):
        xc = xc + xe_ref[j * ns:j * ns + r_rows, :] * convw_ref[j:j + 1, :]
    new_hist = xe_ref[r_rows:r_rows + hist, :]
    xe_ref[0:hist, :] = new_hist

    xcb = xc.astype(BF16)
    r_parts, i_parts = [], []
    for n in range(N_LRU_BLOCKS):
        ri = _dot(xcb[:, n * LRU_BLOCK:(n + 1) * LRU_BLOCK], wax_ref[n])
        r_parts.append(ri[:, :LRU_BLOCK])
        i_parts.append(ri[:, LRU_BLOCK:])
    r_gate = jax.nn.sigmoid(jnp.concatenate(r_parts, axis=1) + ba_ref[...])
    i_gate = jax.nn.sigmoid(jnp.concatenate(i_parts, axis=1) + bx_ref[...])
    log_a = (-LRU_C) * r_gate * jax.nn.softplus(-lam_ref[...])
    a = jnp.exp(log_a)
    a_ref[...] = a
    y = -jnp.tanh(log_a) * (a * a + 1.0)
    b_ref[...] = jnp.where(y > 0.0, y * lax.rsqrt(y), 0.0) * (i_gate * xc)

    h = h_ref[...]
    for t in range(tt):
        h = a_ref[t * ns:(t + 1) * ns, :] * h + b_ref[t * ns:(t + 1) * ns, :]
        hs_ref[t * ns:(t + 1) * ns, :] = h
    h_ref[...] = h
    return hs_ref[...] * jax.nn.gelu(gate)


def _merge_norm(x, gates, attn, rec, wap_ref, wlp_ref, wout_ref, g1_ref, b1_ref, alpha):
    ap = _dot(attn.astype(BF16), wap_ref[...])
    lp = _dot(rec.astype(BF16), wlp_ref[...])
    merged = (jax.nn.sigmoid(gates[:, :D_MODEL]) * ap
              + jax.nn.sigmoid(gates[:, D_MODEL:]) * lp)
    mix = _dot(merged.astype(BF16), wout_ref[...])
    return _layer_norm(alpha * x + mix, g1_ref[...], b1_ref[...])


def _layer_spec(arr, layer):
    nd = arr.ndim
    return pl.BlockSpec((None,) + tuple(arr.shape[1:]), lambda *_: (layer,) + (0,) * (nd - 1),
                        pipeline_mode=pl.Buffered(1))


def _full_spec(shape):
    nd = len(shape)
    return pl.BlockSpec(tuple(shape), lambda *_: (0,) * nd, pipeline_mode=pl.Buffered(1))


_COMPILER_PARAMS = pltpu.CompilerParams(dimension_semantics=("arbitrary",),
                                        vmem_limit_bytes=VMEM_LIMIT_BYTES)


def _mixer_tile(load_x, cos, sin, y_ref, pos_start, sink_ref, win_ref, wap_ref, wlp_ref, wout_ref,
                convw_ref, convb_ref, wax_ref, ba_ref, bx_ref, lam_ref, g1_ref, b1_ref,
                q_s, k_s, v_s, attn_s, kwin_s, vwin_s, xe_s, a_s, b_s, hs_s, h_s, *, layer, ns, tt, alpha):
    r_rows = ns * tt
    n_qg = ATTN_WIDTH // LANES
    proj = _dot(load_x().astype(BF16), win_ref[...])
    qkv = proj[:, 0:V_END]
    cos_q = cos * (HEAD_DIM ** -0.5)
    sin_q = sin * (HEAD_DIM ** -0.5)
    for j in range(n_qg):
        q_s[j, 0:r_rows, :] = _rope(qkv[:, j * LANES:(j + 1) * LANES], cos_q, sin_q)
    k_s[0:r_rows, :] = _rope(qkv[:, Q_END:K_END], cos, sin)
    v_s[0:r_rows, :] = qkv[:, K_END:V_END]

    valid = _window_valid(tt, WINDOW + tt, WINDOW - pos_start)
    sink_cols = [_sink_column(sink_ref, layer, kv, tt) for kv in range(N_KV_HEADS)]
    for b in range(ns):
        rows = pl.ds(b, tt, stride=ns)
        kwin_s[b, WINDOW:WINDOW + tt, :] = k_s[rows, :]
        vwin_s[b, WINDOW:WINDOW + tt, :] = v_s[rows, :]
        keys = kwin_s[b, 0:WINDOW + tt, :]
        vals = vwin_s[b, 0:WINDOW + tt, :]
        kb = keys.astype(BF16)
        vb = vals.astype(BF16)
        q_b = [q_s[j, rows, :].astype(BF16) for j in range(n_qg)]
        heads = [q_b[h // 2][:, (h % 2) * HEAD_DIM:(h % 2 + 1) * HEAD_DIM] for h in range(N_Q_HEADS)]
        outs = []
        for kv in range(N_KV_HEADS):
            kk = kb[:, kv * HEAD_DIM:(kv + 1) * HEAD_DIM]
            vv = vb[:, kv * HEAD_DIM:(kv + 1) * HEAD_DIM]
            qs = jnp.concatenate(heads[Q_PER_KV * kv:Q_PER_KV * (kv + 1)], axis=0)
            o = _softmax_pv(_dot_tb(qs, kk), valid, sink_cols[kv], vv)
            outs.extend(o[g * tt:(g + 1) * tt] for g in range(Q_PER_KV))
        for j in range(n_qg):
            attn_s[j, rows, :] = jnp.concatenate(outs[2 * j:2 * j + 2], axis=1)
        kwin_s[b, 0:WINDOW, :] = keys[tt:tt + WINDOW]
        vwin_s[b, 0:WINDOW, :] = vals[tt:tt + WINDOW]

    tile_rows = pl.ds(0, r_rows)
    rec = _recurrent_branch(proj[:, V_END:XR_END], proj[:, XR_END:GATE_END], xe_s,
                            a_s.at[tile_rows], b_s.at[tile_rows], hs_s.at[tile_rows], h_s,
                            convw_ref, convb_ref, wax_ref, ba_ref, bx_ref, lam_ref, ns=ns, tt=tt)
    attn = jnp.concatenate([attn_s[j, 0:r_rows, :] for j in range(n_qg)], axis=1)
    y_ref[...] = _merge_norm(load_x(), proj[:, GATE_END:D_IN], attn, rec, wap_ref, wlp_ref, wout_ref,
                             g1_ref, b1_ref, alpha)


def _mixer_kernel(sink_ref, xm_ref, x_ref, cosm_ref, sinm_ref, cos_ref, sin_ref,
                  win_ref, wap_ref, wlp_ref, wout_ref,
                  convw_ref, convb_ref, wax_ref, ba_ref, bx_ref, lam_ref, g1_ref, b1_ref,
                  ym_ref, y_ref, kwin_out_ref, vwin_out_ref, conv_out_ref, h_out_ref,
                  q_s, k_s, v_s, attn_s, kwin_s, vwin_s, xe_s, a_s, b_s, hs_s, h_s, *maybe_xt_s,
                  layer, ns, tt, tt_meta, alpha):
    step = pl.program_id(0)
    hist = (CONV_W - 1) * ns
    n_slabs = D_MODEL // LANES
    shared = (sink_ref, win_ref, wap_ref, wlp_ref, wout_ref,
              convw_ref, convb_ref, wax_ref, ba_ref, bx_ref, lam_ref, g1_ref, b1_ref,
              q_s, k_s, v_s, attn_s, kwin_s, vwin_s, xe_s, a_s, b_s, hs_s, h_s)

    @pl.when(step == 0)
    def _():
        kwin_s[:, 0:WINDOW, :] = jnp.zeros((ns, WINDOW, KV_WIDTH), F32)
        vwin_s[:, 0:WINDOW, :] = jnp.zeros((ns, WINDOW, KV_WIDTH), F32)
        xe_s[0:hist, :] = jnp.zeros((hist, D_RNN), F32)
        h_s[...] = jnp.zeros((ns, D_RNN), F32)
        _mixer_tile(lambda: xm_ref[...], cosm_ref[...], sinm_ref[...], ym_ref, 0, *shared,
                    layer=layer, ns=ns, tt=tt_meta, alpha=alpha)

    @pl.when(step > 0)
    def _():
        if maybe_xt_s:
            xt_s, = maybe_xt_s
            for b in range(ns):
                for c in range(n_slabs):
                    xt_s[c, pl.ds(b, tt, stride=ns), :] = x_ref[b, :, c * LANES:(c + 1) * LANES]
            load_x = lambda: jnp.concatenate([xt_s[c] for c in range(n_slabs)], axis=1)
        else:
            load_x = lambda: x_ref[...]
        _mixer_tile(load_x, cos_ref[...], sin_ref[...], y_ref, tt_meta + (step - 1) * tt, *shared,
                    layer=layer, ns=ns, tt=tt, alpha=alpha)

    @pl.when(step == pl.num_programs(0) - 1)
    def _():
        kwin_out_ref[...] = kwin_s[:, 0:WINDOW, :]
        vwin_out_ref[...] = vwin_s[:, 0:WINDOW, :]
        conv_out_ref[...] = xe_s[0:hist, :]
        h_out_ref[...] = h_s[...]


def _mixer_call(xm, x, tables_m, tables_p, wts, *, layer, ns, tt, alpha):
    batch_major = x.ndim == 3
    rows = x.shape[0] * x.shape[1] if batch_major else x.shape[0]
    r_rows = ns * tt
    n_tiles = rows // r_rows
    rows_m = xm.shape[0]
    tt_meta = rows_m // ns
    hist = (CONV_W - 1) * ns

    def row_spec(c):
        return pl.BlockSpec((r_rows, c), lambda i: (jnp.maximum(i - 1, 0), 0))

    if batch_major:
        x_spec = pl.BlockSpec((ns, tt, D_MODEL), lambda i: (0, jnp.maximum(i - 1, 0), 0))
    else:
        x_spec = row_spec(D_MODEL)
    weights = [wts[k] for k in ('w_in', 'w_attn_proj', 'w_lru_proj', 'w_out', 'conv_w', 'conv_b', 'w_ax',
                                'lru_ba', 'lru_bx', 'lru_lambda', 'ln1_g', 'ln1_b')]
    in_specs = ([pl.BlockSpec(memory_space=pltpu.SMEM), _full_spec(xm.shape), x_spec,
                 _full_spec(tables_m[0].shape), _full_spec(tables_m[1].shape), row_spec(LANES), row_spec(LANES)]
                + [_layer_spec(w, layer) for w in weights])
    out_shape = (jax.ShapeDtypeStruct((rows_m, D_MODEL), F32),
                 jax.ShapeDtypeStruct((rows, D_MODEL), F32),
                 jax.ShapeDtypeStruct((ns, WINDOW, KV_WIDTH), F32),
                 jax.ShapeDtypeStruct((ns, WINDOW, KV_WIDTH), F32),
                 jax.ShapeDtypeStruct((hist, D_RNN), F32),
                 jax.ShapeDtypeStruct((ns, D_RNN), F32))
    out_specs = (pl.BlockSpec((rows_m, D_MODEL), lambda i: (0, 0)),
                 row_spec(D_MODEL),
                 pl.BlockSpec((ns, WINDOW, KV_WIDTH), lambda i: (0, 0, 0)),
                 pl.BlockSpec((ns, WINDOW, KV_WIDTH), lambda i: (0, 0, 0)),
                 pl.BlockSpec((hist, D_RNN), lambda i: (0, 0)),
                 pl.BlockSpec((ns, D_RNN), lambda i: (0, 0)))
    scratch = [pltpu.VMEM((ATTN_WIDTH // LANES, r_rows, LANES), F32),
               pltpu.VMEM((r_rows, KV_WIDTH), F32),
               pltpu.VMEM((r_rows, KV_WIDTH), F32),
               pltpu.VMEM((ATTN_WIDTH // LANES, r_rows, LANES), F32),
               pltpu.VMEM((ns, WINDOW + tt, KV_WIDTH), F32),
               pltpu.VMEM((ns, WINDOW + tt, KV_WIDTH), F32),
               pltpu.VMEM((hist + r_rows, D_RNN), F32),
               pltpu.VMEM((r_rows, D_RNN), F32),
               pltpu.VMEM((r_rows, D_RNN), F32),
               pltpu.VMEM((r_rows, D_RNN), F32),
               pltpu.VMEM((ns, D_RNN), F32)]
    if batch_major:
        scratch.append(pltpu.VMEM((D_MODEL // LANES, r_rows, LANES), F32))
    outs = pl.pallas_call(
        functools.partial(_mixer_kernel, layer=layer, ns=ns, tt=tt, tt_meta=tt_meta, alpha=alpha),
        grid=(n_tiles + 1,),
        in_specs=in_specs,
        out_specs=out_specs,
        out_shape=out_shape,
        scratch_shapes=scratch,
        compiler_params=_COMPILER_PARAMS,
        name=f"mixer_ns{ns}_tt{tt}",
    )(wts['sinks'], xm, x, *tables_m, *tables_p, *weights)
    return outs[0], outs[1], outs[2:]


def _ffn_tile(x, win_ref, wout_ref, g_ref, b_ref, alpha):
    u = _dot(x.astype(BF16), win_ref[...])
    hmid = jax.nn.silu(u[:, :D_FF]) * u[:, D_FF:]
    f = _dot(hmid.astype(BF16), wout_ref[...])
    return _layer_norm(alpha * x + f, g_ref[...], b_ref[...])


def _ffn_kernel(x_ref, xa_ref, xb_ref, win_ref, wout_ref, g_ref, b_ref, y_ref, ya_ref, yb_ref,
                *maybe_yt_s, alpha):
    step = pl.program_id(0)
    n_main = pl.num_programs(0) - 2

    @pl.when(step < n_main)
    def _():
        y = _ffn_tile(x_ref[...], win_ref, wout_ref, g_ref, b_ref, alpha)
        if maybe_yt_s:
            yt_s, = maybe_yt_s
            ns, tt, _ = y_ref.shape
            for c in range(D_MODEL // LANES):
                yt_s[c] = y[:, c * LANES:(c + 1) * LANES]
            for b in range(ns):
                for c in range(D_MODEL // LANES):
                    y_ref[b, :, c * LANES:(c + 1) * LANES] = yt_s[c, pl.ds(b, tt, stride=ns), :]
        else:
            y_ref[...] = y

    @pl.when(step == n_main)
    def _():
        ya_ref[...] = _ffn_tile(xa_ref[...], win_ref, wout_ref, g_ref, b_ref, alpha)

    @pl.when(step == n_main + 1)
    def _():
        yb_ref[...] = _ffn_tile(xb_ref[...], win_ref, wout_ref, g_ref, b_ref, alpha)


def _ffn_call(x, x_a, x_b, wts, *, layer, tm, alpha, batch_major_out=None):
    rows = x.shape[0]
    n_main = rows // tm
    weights = [wts[k] for k in ('w_ffn_in', 'w_ffn_out', 'ln2_g', 'ln2_b')]
    main_idx = lambda i: jnp.minimum(i, n_main - 1)
    if batch_major_out is None:
        out_spec = pl.BlockSpec((tm, D_MODEL), lambda i: (main_idx(i), 0))
        out_shape = jax.ShapeDtypeStruct((rows, D_MODEL), F32)
        scratch = []
    else:
        ns = batch_major_out
        out_spec = pl.BlockSpec((ns, tm // ns, D_MODEL), lambda i: (0, main_idx(i), 0))
        out_shape = jax.ShapeDtypeStruct((ns, rows // ns, D_MODEL), F32)
        scratch = [pltpu.VMEM((D_MODEL // LANES, tm, LANES), F32)]
    return pl.pallas_call(
        functools.partial(_ffn_kernel, alpha=alpha),
        grid=(n_main + 2,),
        in_specs=([pl.BlockSpec((tm, D_MODEL), lambda i: (main_idx(i), 0)),
                   _full_spec(x_a.shape), _full_spec(x_b.shape)]
                  + [_layer_spec(w, layer) for w in weights]),
        out_specs=(out_spec, pl.BlockSpec(x_a.shape, lambda i: (0, 0)), pl.BlockSpec(x_b.shape, lambda i: (0, 0))),
        out_shape=(out_shape, jax.ShapeDtypeStruct(x_a.shape, F32), jax.ShapeDtypeStruct(x_b.shape, F32)),
        scratch_shapes=scratch,
        compiler_params=_COMPILER_PARAMS,
        name=f"ffn_tm{tm}",
    )(x, x_a, x_b, *weights)


def _sample_pre_kernel(x_ref, cos_ref, sin_ref, win_ref,
                       convw_ref, convb_ref, wax_ref, ba_ref, bx_ref, lam_ref,
                       conv0_ref, h0_ref,
                       q_ref, k_ref, v_ref, rec_ref, conv_out_ref, h_out_ref,
                       xe_s, a_s, b_s, hs_s, h_s, *, ns, tt):
    hist = (CONV_W - 1) * ns
    xe_s[0:hist, :] = conv0_ref[...]
    h_s[...] = h0_ref[...]
    xb = x_ref[...].astype(BF16)
    proj = _dot(xb, win_ref[:, 0:GATE_END])
    qkv = proj[:, 0:V_END]
    cos = cos_ref[...]
    sin = sin_ref[...]
    q_ref[...] = _rope(qkv[:, :Q_END], cos, sin)
    k_ref[...] = _rope(qkv[:, Q_END:K_END], cos, sin)
    v_ref[...] = qkv[:, K_END:V_END]
    xg = proj[:, V_END:GATE_END]
    rec_ref[...] = _recurrent_branch(xg[:, :D_RNN], xg[:, D_RNN:], xe_s, a_s, b_s, hs_s, h_s,
                                     convw_ref, convb_ref, wax_ref, ba_ref, bx_ref, lam_ref,
                                     ns=ns, tt=tt)
    conv_out_ref[...] = xe_s[0:hist, :]
    h_out_ref[...] = h_s[...]


def _sample_pre_call(x, cos, sin, wts, conv_all, h_all, *, layer, ns, tt):
    rows = ns * tt
    hist = (CONV_W - 1) * ns
    weights = [wts[k] for k in ('w_in', 'conv_w', 'conv_b', 'w_ax', 'lru_ba', 'lru_bx', 'lru_lambda')]
    out_shape = (jax.ShapeDtypeStruct((rows, ATTN_WIDTH), F32),
                 jax.ShapeDtypeStruct((rows, KV_WIDTH), F32),
                 jax.ShapeDtypeStruct((rows, KV_WIDTH), F32),
                 jax.ShapeDtypeStruct((rows, D_RNN), F32),
                 jax.ShapeDtypeStruct((hist, D_RNN), F32),
                 jax.ShapeDtypeStruct((ns, D_RNN), F32))
    return pl.pallas_call(
        functools.partial(_sample_pre_kernel, ns=ns, tt=tt),
        grid=(1,),
        in_specs=([_full_spec(a.shape) for a in (x, cos, sin)]
                  + [_layer_spec(w, layer) for w in weights]
                  + [_layer_spec(conv_all, layer), _layer_spec(h_all, layer)]),
        out_specs=tuple(pl.BlockSpec(s.shape, lambda i, _n=len(s.shape): (0,) * _n) for s in out_shape),
        out_shape=out_shape,
        scratch_shapes=[pltpu.VMEM((hist + rows, D_RNN), F32),
                        pltpu.VMEM((rows, D_RNN), F32),
                        pltpu.VMEM((rows, D_RNN), F32),
                        pltpu.VMEM((rows, D_RNN), F32),
                        pltpu.VMEM((ns, D_RNN), F32)],
        compiler_params=_COMPILER_PARAMS,
        name="sample_pre",
    )(x, cos, sin, *weights, conv_all, h_all)


def _sample_attn_kernel(sink_ref, q_ref, knt_ref, vnt_ref, kc_ref, vc_ref,
                        kprev_ref, vprev_ref, o_ref, ko_ref, vo_ref, *, layer, nb, tt):
    del kprev_ref, vprev_ref
    rows = N_Q_HEADS * tt
    scale = HEAD_DIM ** -0.5
    t_row = lax.broadcasted_iota(jnp.int32, (rows, 2 * WINDOW), 0) & (tt - 1)
    col = lax.broadcasted_iota(jnp.int32, (rows, 2 * WINDOW), 1)
    valid_old = (col < WINDOW) & (col > t_row) & (col >= WINDOW - PAST_LEN)
    lane = lax.broadcasted_iota(jnp.int32, (KV_WIDTH, WINDOW), 1)
    keep_old = lane < WINDOW - tt
    row1 = lax.broadcasted_iota(jnp.int32, (rows, 1), 0)
    sink_col = jnp.full((rows, 1), sink_ref[layer, N_Q_HEADS - 1], F32)
    for h in range(N_Q_HEADS - 2, -1, -1):
        sink_col = jnp.where(row1 < (h + 1) * tt, sink_ref[layer, h], sink_col)
    knt = knt_ref[...]
    vnt = vnt_ref[...]
    knt_b = knt.astype(BF16)
    vnt_b = vnt.astype(BF16)
    for s in range(nb):
        k_old = kc_ref[s]
        v_old = vc_ref[s]
        lo = WINDOW + s * tt
        valid = valid_old | ((col >= lo) & (col <= lo + t_row))
        k_all = jnp.concatenate([k_old.astype(BF16), knt_b], axis=1)
        v_all = jnp.concatenate([v_old.astype(BF16), vnt_b], axis=1)
        sc = jnp.where(valid, _dot(q_ref[s].astype(BF16), k_all) * scale, NEG_INF)
        m = jnp.maximum(jnp.max(sc, axis=-1, keepdims=True), sink_col)
        p = jnp.exp(sc - m)
        denom = jnp.sum(p, axis=-1, keepdims=True) + jnp.exp(sink_col - m)
        o_ref[s] = _dot_tb(p.astype(BF16), v_all) * (1.0 / denom)
        new_shift = (WINDOW - tt - tt * s) % WINDOW
        ko_ref[s] = jnp.where(keep_old, pltpu.roll(k_old, WINDOW - tt, 1), pltpu.roll(knt, new_shift, 1))
        vo_ref[s] = jnp.where(keep_old, pltpu.roll(v_old, WINDOW - tt, 1), pltpu.roll(vnt, new_shift, 1))


def _sample_attn_call(sinks, qz, knt, vnt, kc_all, vc_all, kprev, vprev, *, layer, tt):
    nseq = qz.shape[0]
    nb = WINDOW // tt

    q_spec = pl.BlockSpec((nb,) + tuple(qz.shape[1:]), lambda i: (i, 0, 0))
    cache_spec = pl.BlockSpec((None, nb, KV_WIDTH, WINDOW), lambda i: (layer, i, 0, 0))
    col_spec = pl.BlockSpec((KV_WIDTH, WINDOW), lambda i: (0, i))
    any_spec = pl.BlockSpec(memory_space=pl.ANY)
    out_shape = (jax.ShapeDtypeStruct(qz.shape, F32),
                 jax.ShapeDtypeStruct(kprev.shape, F32),
                 jax.ShapeDtypeStruct(vprev.shape, F32))
    return pl.pallas_call(
        functools.partial(_sample_attn_kernel, layer=layer, nb=nb, tt=tt),
        grid=(nseq // nb,),
        in_specs=[pl.BlockSpec(memory_space=pltpu.SMEM), q_spec, col_spec, col_spec,
                  cache_spec, cache_spec, any_spec, any_spec],
        out_specs=(q_spec, cache_spec, cache_spec),
        out_shape=out_shape,
        input_output_aliases={6: 1, 7: 2},
        compiler_params=_COMPILER_PARAMS,
        name="sample_attn",
    )(sinks, qz, knt, vnt, kc_all, vc_all, kprev, vprev)


def _sample_post_kernel(x_ref, attn_ref, rec_ref, win_ref, wap_ref, wlp_ref, wout_ref,
                        g1_ref, b1_ref, y_ref, *, alpha):
    x = x_ref[...]
    gates = _dot(x.astype(BF16), win_ref[:, GATE_END:D_IN])
    y_ref[...] = _merge_norm(x, gates, attn_ref[...], rec_ref[...],
                             wap_ref, wlp_ref, wout_ref, g1_ref, b1_ref, alpha)


def _sample_post_call(x, attn, rec, wts, *, layer, alpha):
    weights = [wts[k] for k in ('w_in', 'w_attn_proj', 'w_lru_proj', 'w_out', 'ln1_g', 'ln1_b')]
    return pl.pallas_call(
        functools.partial(_sample_post_kernel, alpha=alpha),
        grid=(1,),
        in_specs=[_full_spec(a.shape) for a in (x, attn, rec)] + [_layer_spec(w, layer) for w in weights],
        out_specs=pl.BlockSpec(x.shape, lambda i: (0, 0)),
        out_shape=jax.ShapeDtypeStruct(x.shape, F32),
        compiler_params=_COMPILER_PARAMS,
        name="sample_post",
    )(x, attn, rec, *weights)


def _rope_tables(pos, repeat):
    half = HEAD_DIM // 2
    inv = ROPE_THETA ** (-jnp.arange(half, dtype=F32) / half)
    ang = pos.astype(F32)[:, None] * inv[None, :]
    cos = jnp.cos(ang)
    sin = jnp.sin(ang)
    cos_t = jnp.tile(cos, (1, LANES // half))
    sin_t = jnp.tile(jnp.concatenate([-sin, sin], axis=1), (1, LANES // HEAD_DIM))
    return jnp.repeat(cos_t, repeat, axis=0), jnp.repeat(sin_t, repeat, axis=0)


def _prepare_weights(w_in, w_attn_proj, w_lru_proj, w_out, attn_sinks, conv_w, conv_b,
                     lru_wa, lru_ba, lru_wx, lru_bx, lru_lambda, ln1_g, ln1_b,
                     w_ffn_in, w_ffn_out, ln2_g, ln2_b):
    depth = w_in.shape[0]
    row = lambda v: v.reshape(depth, 1, -1).astype(F32)
    return dict(
        w_in=w_in.astype(BF16),
        w_attn_proj=w_attn_proj.astype(BF16),
        w_lru_proj=w_lru_proj.astype(BF16),
        w_out=w_out.astype(BF16),
        sinks=attn_sinks.astype(F32),
        conv_w=conv_w.astype(F32),
        conv_b=row(conv_b),
        w_ax=jnp.concatenate([lru_wa, lru_wx], axis=-1).astype(BF16),
        lru_ba=row(lru_ba), lru_bx=row(lru_bx), lru_lambda=row(lru_lambda),
        ln1_g=row(ln1_g), ln1_b=row(ln1_b),
        w_ffn_in=w_ffn_in.astype(BF16),
        w_ffn_out=w_ffn_out.astype(BF16),
        ln2_g=row(ln2_g), ln2_b=row(ln2_b),
    )


def kernel(x_prompt, x_sample, cache_win_k, cache_win_v, state_conv, state_lru, meta_tokens, w_in, w_attn_proj, w_lru_proj, w_out, attn_sinks, conv_w, conv_b, lru_wa, lru_ba, lru_wx, lru_bx, lru_lambda, ln1_g, ln1_b, w_ffn_in, w_ffn_out, ln2_g, ln2_b):
    depth = w_in.shape[0]
    alpha = (2 * depth) ** 0.25
    batch, seq, d = x_prompt.shape
    dec_batch, dec_seq, _ = x_sample.shape
    wts = _prepare_weights(w_in, w_attn_proj, w_lru_proj, w_out, attn_sinks, conv_w, conv_b,
                           lru_wa, lru_ba, lru_wx, lru_bx, lru_lambda, ln1_g, ln1_b,
                           w_ffn_in, w_ffn_out, ln2_g, ln2_b)

    xp = x_prompt
    xm = jnp.broadcast_to(meta_tokens.astype(F32)[:, None, :], (N_META, batch, d)).reshape(N_META * batch, d)
    xs = jnp.transpose(x_sample, (1, 0, 2)).reshape(dec_seq * dec_batch, d)

    cos_m, sin_m = _rope_tables(jnp.arange(N_META, dtype=jnp.int32), batch)
    cos_p, sin_p = _rope_tables(N_META + jnp.arange(seq, dtype=jnp.int32), batch)
    cos_s, sin_s = _rope_tables(PAST_LEN + jnp.arange(dec_seq, dtype=jnp.int32), dec_batch)

    conv_all = jnp.transpose(state_conv, (0, 2, 1, 3)).reshape(depth, (CONV_W - 1) * dec_batch, D_RNN)
    kc_all = jnp.transpose(cache_win_k, (0, 1, 3, 4, 2)).reshape(depth, dec_batch, KV_WIDTH, WINDOW)
    vc_all = jnp.transpose(cache_win_v, (0, 1, 3, 4, 2)).reshape(depth, dec_batch, KV_WIDTH, WINDOW)
    win_k_s = jnp.zeros(kc_all.shape, F32)
    win_v_s = jnp.zeros(vc_all.shape, F32)

    outs = [[] for _ in range(6)]
    for l in range(depth):
        xm1, xp1, (kw, vw, cv, hh) = _mixer_call(xm, xp, (cos_m, sin_m), (cos_p, sin_p), wts,
                                                  layer=l, ns=batch, tt=PROMPT_TT, alpha=alpha)
        outs[0].append(kw.reshape(batch, WINDOW, N_KV_HEADS, HEAD_DIM))
        outs[1].append(vw.reshape(batch, WINDOW, N_KV_HEADS, HEAD_DIM))
        outs[2].append(jnp.transpose(cv.reshape(CONV_W - 1, batch, D_RNN), (1, 0, 2)))
        outs[3].append(hh)

        q, k, v, rec, cv_s, hh_s = _sample_pre_call(xs, cos_s, sin_s, wts, conv_all, state_lru,
                                                    layer=l, ns=dec_batch, tt=dec_seq)
        q5 = jnp.transpose(q.reshape(dec_seq, dec_batch, N_KV_HEADS, Q_PER_KV, HEAD_DIM), (1, 2, 3, 0, 4))
        own_group = jnp.eye(N_KV_HEADS, dtype=bool)[None, :, None, None, :, None]
        qz = jnp.where(own_group, q5[:, :, :, :, None, :], 0.0).reshape(dec_batch, N_Q_HEADS * dec_seq, KV_WIDTH)
        knt = jnp.transpose(k.reshape(dec_seq, dec_batch, KV_WIDTH), (2, 1, 0)).reshape(KV_WIDTH, dec_batch * dec_seq)
        vnt = jnp.transpose(v.reshape(dec_seq, dec_batch, KV_WIDTH), (2, 1, 0)).reshape(KV_WIDTH, dec_batch * dec_seq)
        oz, win_k_s, win_v_s = _sample_attn_call(wts['sinks'], qz, knt, vnt, kc_all, vc_all,
                                                 win_k_s, win_v_s, layer=l, tt=dec_seq)
        o6 = oz.reshape(dec_batch, N_KV_HEADS, Q_PER_KV, dec_seq, N_KV_HEADS, HEAD_DIM)
        o5 = jnp.stack([o6[:, kv, :, :, kv, :] for kv in range(N_KV_HEADS)], axis=1)
        attn = jnp.transpose(o5, (3, 0, 1, 2, 4)).reshape(dec_seq * dec_batch, ATTN_WIDTH)
        xs1 = _sample_post_call(xs, attn, rec, wts, layer=l, alpha=alpha)
        xp, xs, xm = _ffn_call(xp1, xs1, xm1, wts, layer=l, tm=FFN_TM, alpha=alpha,
                               batch_major_out=batch if l == depth - 1 else None)
        outs[4].append(cv_s.reshape(CONV_W - 1, dec_batch, D_RNN))
        outs[5].append(hh_s)

    y_prompt = xp
    y_sample = jnp.transpose(xs.reshape(dec_seq, dec_batch, d), (1, 0, 2))
    st = [jnp.stack(o) for o in outs]
    win_shape = (depth, dec_batch, N_KV_HEADS, HEAD_DIM, WINDOW)
    win_k_sample = jnp.transpose(win_k_s.reshape(win_shape), (0, 1, 4, 2, 3))
    win_v_sample = jnp.transpose(win_v_s.reshape(win_shape), (0, 1, 4, 2, 3))
    conv_sample = jnp.transpose(st[4], (0, 2, 1, 3))
    return (y_prompt, y_sample, st[0], st[1], st[2], st[3], win_k_sample, win_v_sample, conv_sample, st[5])
```
